```python
import math, functools
import jax, jax.numpy as jnp
from jax import lax
import numpy as np

D_MODEL = 1024
BATCH = 4
SEQ = 8192
DEPTH = 2
DEC_BATCH = 32
DEC_SEQ = 4
PAST_LEN = 16384
PAGE_SIZE = 128

N_PAGES = PAST_LEN // PAGE_SIZE
N_POOL_PAGES = (DEC_BATCH * N_PAGES * 5) // 4

MIX_WIDTH = D_MODEL
A_WIDTH = MIX_WIDTH // 4
B_WIDTH = MIX_WIDTH // 2
C_WIDTH = MIX_WIDTH - A_WIDTH - B_WIDTH
QK_DIM = 64
V_DIM = 2 * QK_DIM
N_DIFF_HEADS = B_WIDTH // V_DIM
Q_COLS = N_DIFF_HEADS * 2 * QK_DIM
V_COLS = N_DIFF_HEADS * V_DIM
IN_COLS = 2 * A_WIDTH + 2 * Q_COLS + V_COLS + 3 * C_WIDTH
ROT_DIM = QK_DIM // 4
ROPE_THETA = 500000.0
KA = 31
KC = 3
D_FF = 4 * D_MODEL
Q_BLOCK = 128
EPS = 1e-6
NEG_INF = -1e30

kernel_name = 'hymba_diffattn_conformer_shortconv_step'


def rms_norm(x, w):
    xf = x.astype(jnp.float32)
    y = xf * lax.rsqrt(jnp.mean(xf * xf, axis=-1, keepdims=True) + EPS)
    return (y * w.astype(jnp.float32)).astype(x.dtype)


def layer_norm(x, w, b):
    xf = x.astype(jnp.float32)
    mu = jnp.mean(xf, axis=-1, keepdims=True)
    var = jnp.mean(jnp.square(xf - mu), axis=-1, keepdims=True)
    y = (xf - mu) * lax.rsqrt(var + EPS) * w.astype(jnp.float32) + b.astype(jnp.float32)
    return y.astype(x.dtype)


def rope_partial(x, pos):
    half = ROT_DIM // 2
    inv_freq = ROPE_THETA ** (-jnp.arange(half, dtype=jnp.float32) / half)
    ang = pos.astype(jnp.float32)[:, None] * inv_freq[None, :]
    cos = jnp.cos(ang)[:, None, None, :]
    sin = jnp.sin(ang)[:, None, None, :]
    xr = x[..., :ROT_DIM].astype(jnp.float32)
    x1, x2 = xr[..., :half], xr[..., half:]
    rot = jnp.concatenate([x1 * cos - x2 * sin, x2 * cos + x1 * sin], axis=-1).astype(x.dtype)
    return jnp.concatenate([rot, x[..., ROT_DIM:]], axis=-1)


def depthwise_conv_valid(xp, w):
    return lax.conv_general_dilated(
        xp, w[:, None, :].astype(xp.dtype), window_strides=(1,), padding='VALID',
        dimension_numbers=('NWC', 'WIO', 'NWC'), feature_group_count=xp.shape[-1])


def diff_attention_core(q, k, v, q_pos, k_pos, lam):
    s = jnp.einsum('bqhcd,bkhcd->bhcqk', q.astype(jnp.float32), k.astype(jnp.float32)) * (QK_DIM ** -0.5)
    mask = k_pos[None, :] <= q_pos[:, None]
    s = jnp.where(mask, s, NEG_INF)
    p = jax.nn.softmax(s, axis=-1)
    p = p[:, :, 0] - lam * p[:, :, 1]
    o = jnp.einsum('bhqk,bkhv->bqhv', p, v.astype(jnp.float32))
    return o.astype(v.dtype)


def prompt_attention(q, k, v, lam):
    b, t = q.shape[0], q.shape[1]
    nb = t // Q_BLOCK
    qb = q.reshape(b, nb, Q_BLOCK, N_DIFF_HEADS, 2, QK_DIM).swapaxes(0, 1)
    k_pos = jnp.arange(t)

    def one_block(args):
        qi, i = args
        q_pos = i * Q_BLOCK + jnp.arange(Q_BLOCK)
        return diff_attention_core(qi, k, v, q_pos, k_pos, lam)

    o = lax.map(one_block, (qb, jnp.arange(nb)))
    return o.swapaxes(0, 1).reshape(b, t, N_DIFF_HEADS, V_DIM)


def sample_attention(q, k_new, v_new, cache_k, cache_v, page_table, layer, lam):
    past = page_table.shape[1] * PAGE_SIZE
    tn = q.shape[1]
    q_pos = past + jnp.arange(tn)
    k_pos = jnp.arange(past + tn)

    def one_seq(args):
        pt, qs, ks, vs = args
        kp = cache_k[layer, pt].reshape(past, N_DIFF_HEADS, 2, QK_DIM)
        vp = cache_v[layer, pt].reshape(past, N_DIFF_HEADS, V_DIM)
        k_all = jnp.concatenate([kp, ks.astype(kp.dtype)], axis=0)[None]
        v_all = jnp.concatenate([vp, vs.astype(vp.dtype)], axis=0)[None]
        return diff_attention_core(qs[None], k_all, v_all, q_pos, k_pos, lam)[0]

    return lax.map(one_seq, (page_table, q, k_new, v_new))


def layer_forward(x, c, pos, p, lam_init, a_hist, c_hist, attend):
    b, t = x.shape[0], x.shape[1]
    mod = jnp.einsum('bd,dn->bn', jax.nn.silu(c), p['w_ada']) + p['b_ada']
    sh1, sc1, g1, sh2, sc2, g2 = jnp.split(mod[:, None, :], 6, axis=-1)
    u = rms_norm(x, p['norm1_w']) * (1 + sc1) + sh1
    z = jnp.einsum('btd,dn->btn', u, p['w_in'])
    offs = [A_WIDTH, 2 * A_WIDTH, 2 * A_WIDTH + Q_COLS, 2 * A_WIDTH + 2 * Q_COLS,
            2 * A_WIDTH + 2 * Q_COLS + V_COLS, 2 * A_WIDTH + 2 * Q_COLS + V_COLS + C_WIDTH,
            2 * A_WIDTH + 2 * Q_COLS + V_COLS + 2 * C_WIDTH]
    a_val, a_gate, q, k, v, g_b, g_c, g_h = jnp.split(z, offs, axis=-1)

    a = a_val * jax.nn.sigmoid(a_gate)
    a_full = jnp.concatenate([a_hist.astype(a.dtype), a], axis=1)
    ya = depthwise_conv_valid(a_full, p['conv_a_w']) + p['conv_a_b']
    ya = jax.nn.silu(layer_norm(ya, p['ln_a_w'], p['ln_a_b']))

    g = g_c * g_h
    g_full = jnp.concatenate([c_hist.astype(g.dtype), g], axis=1)
    yc = g_b * depthwise_conv_valid(g_full, p['conv_c_w'])

    q = rope_partial(rms_norm(q.reshape(b, t, N_DIFF_HEADS, 2, QK_DIM), p['q_norm_w']), pos)
    k = rope_partial(rms_norm(k.reshape(b, t, N_DIFF_HEADS, 2, QK_DIM), p['k_norm_w']), pos)
    v = v.reshape(b, t, N_DIFF_HEADS, V_DIM)
    f32 = jnp.float32
    lam = (jnp.exp(jnp.sum(p['lambda_q1'].astype(f32) * p['lambda_k1'].astype(f32)))
           - jnp.exp(jnp.sum(p['lambda_q2'].astype(f32) * p['lambda_k2'].astype(f32))) + lam_init)
    o = attend(q, k, v, lam)
    o = rms_norm(o, p['subln_w']) * (1 - lam_init)
    yb = o.reshape(b, t, B_WIDTH)

    y_mix = jnp.einsum('btm,md->btd', jnp.concatenate([ya, yb, yc], axis=-1), p['w_out'])
    x = x + g1 * y_mix
    h = rms_norm(x, p['norm2_w']) * (1 + sc2) + sh2
    hid = jnp.square(jax.nn.relu(jnp.einsum('btd,df->btf', h, p['w_mlp_up'])))
    x = x + g2 * jnp.einsum('btf,fd->btd', hid, p['w_mlp_down'])
    return x, k, v, a_full[:, -(KA - 1):], g_full[:, -(KC - 1):]


def setup_inputs(seed: int = 0) -> dict:
    key = jax.random.key(seed)
    ks = jax.random.split(key, 32)
    f32 = jnp.float32

    def nrm(k, shape, scale):
        return jax.random.normal(k, shape, f32) * scale

    page_table = jax.random.permutation(ks[6], N_POOL_PAGES)[:DEC_BATCH * N_PAGES]
    page_table = page_table.reshape(DEC_BATCH, N_PAGES).astype(jnp.int32)
    return {
        'x_prompt': nrm(ks[0], (BATCH, SEQ, D_MODEL), 1.0),
        'x_sample': nrm(ks[1], (DEC_BATCH, DEC_SEQ, D_MODEL), 1.0),
        'cache_k': nrm(ks[2], (DEPTH, N_POOL_PAGES, PAGE_SIZE, N_DIFF_HEADS, 2, QK_DIM), 1.0),
        'cache_v': nrm(ks[3], (DEPTH, N_POOL_PAGES, PAGE_SIZE, N_DIFF_HEADS, V_DIM), 1.0),
        'state_conv_a': nrm(ks[4], (DEPTH, DEC_BATCH, KA - 1, A_WIDTH), 0.5),
        'state_conv_c': nrm(ks[5], (DEPTH, DEC_BATCH, KC - 1, C_WIDTH), 0.5),
        'page_table': page_table,
        'c_prompt': nrm(ks[7], (BATCH, D_MODEL), 1.0),
        'c_sample': nrm(ks[8], (DEC_BATCH, D_MODEL), 1.0),
        'norm1_w': 1.0 + nrm(ks[9], (DEPTH, D_MODEL), 0.05),
        'norm2_w': 1.0 + nrm(ks[10], (DEPTH, D_MODEL), 0.05),
        'w_ada': nrm(ks[11], (DEPTH, D_MODEL, 6 * D_MODEL), 0.5 * D_MODEL ** -0.5),
        'b_ada': nrm(ks[12], (DEPTH, 6 * D_MODEL), 0.02),
        'w_in': nrm(ks[13], (DEPTH, D_MODEL, IN_COLS), D_MODEL ** -0.5),
        'conv_a_w': nrm(ks[14], (DEPTH, KA, A_WIDTH), KA ** -0.5),
        'conv_a_b': nrm(ks[15], (DEPTH, A_WIDTH), 0.02),
        'ln_a_w': 1.0 + nrm(ks[16], (DEPTH, A_WIDTH), 0.05),
        'ln_a_b': nrm(ks[17], (DEPTH, A_WIDTH), 0.02),
        'q_norm_w': 1.0 + nrm(ks[18], (DEPTH, QK_DIM), 0.05),
        'k_norm_w': 1.0 + nrm(ks[19], (DEPTH, QK_DIM), 0.05),
        'lambda_q1': nrm(ks[20], (DEPTH, QK_DIM), 0.1),
        'lambda_k1': nrm(ks[21], (DEPTH, QK_DIM), 0.1),
        'lambda_q2': nrm(ks[22], (DEPTH, QK_DIM), 0.1),
        'lambda_k2': nrm(ks[23], (DEPTH, QK_DIM), 0.1),
        'subln_w': 1.0 + nrm(ks[24], (DEPTH, V_DIM), 0.05),
        'conv_c_w': nrm(ks[25], (DEPTH, KC, C_WIDTH), KC ** -0.5),
        'w_out': nrm(ks[26], (DEPTH, MIX_WIDTH, D_MODEL), MIX_WIDTH ** -0.5),
        'w_mlp_up': nrm(ks[27], (DEPTH, D_MODEL, D_FF), D_MODEL ** -0.5),
        'w_mlp_down': nrm(ks[28], (DEPTH, D_FF, D_MODEL), D_FF ** -0.5),
    }


def reference(x_prompt, x_sample, cache_k, cache_v, state_conv_a, state_conv_c, page_table,
              c_prompt, c_sample, norm1_w, norm2_w, w_ada, b_ada, w_in, conv_a_w, conv_a_b,
              ln_a_w, ln_a_b, q_norm_w, k_norm_w, lambda_q1, lambda_k1, lambda_q2, lambda_k2,
              subln_w, conv_c_w, w_out, w_mlp_up, w_mlp_down):
    past = page_table.shape[1] * PAGE_SIZE
    pos_p = jnp.arange(x_prompt.shape[1])
    pos_s = past + jnp.arange(x_sample.shape[1])
    xp, xs = x_prompt, x_sample
    kp_l, vp_l, ap_l, cp_l, ks_l, vs_l, as_l, cs_l = [], [], [], [], [], [], [], []
    for l in range(DEPTH):
        p = {'norm1_w': norm1_w[l], 'norm2_w': norm2_w[l], 'w_ada': w_ada[l], 'b_ada': b_ada[l],
             'w_in': w_in[l], 'conv_a_w': conv_a_w[l], 'conv_a_b': conv_a_b[l],
             'ln_a_w': ln_a_w[l], 'ln_a_b': ln_a_b[l], 'q_norm_w': q_norm_w[l],
             'k_norm_w': k_norm_w[l], 'lambda_q1': lambda_q1[l], 'lambda_k1': lambda_k1[l],
             'lambda_q2': lambda_q2[l], 'lambda_k2': lambda_k2[l], 'subln_w': subln_w[l],
             'conv_c_w': conv_c_w[l], 'w_out': w_out[l], 'w_mlp_up': w_mlp_up[l],
             'w_mlp_down': w_mlp_down[l]}
        lam_init = 0.8 - 0.6 * math.exp(-0.3 * l)
        a_hist0 = jnp.zeros((xp.shape[0], KA - 1, A_WIDTH), xp.dtype)
        c_hist0 = jnp.zeros((xp.shape[0], KC - 1, C_WIDTH), xp.dtype)
        xp, kp, vp, ap, cp = layer_forward(xp, c_prompt, pos_p, p, lam_init, a_hist0, c_hist0,
                                           prompt_attention)
        attend_s = functools.partial(sample_attention, cache_k=cache_k, cache_v=cache_v,
                                     page_table=page_table, layer=l)
        xs, kn, vn, an, cn = layer_forward(xs, c_sample, pos_s, p, lam_init, state_conv_a[l],
                                           state_conv_c[l],
                                           lambda q, k, v, lam: attend_s(q, k, v, lam=lam))
        kp_l.append(kp); vp_l.append(vp); ap_l.append(ap); cp_l.append(cp)
        ks_l.append(kn); vs_l.append(vn); as_l.append(an); cs_l.append(cn)
    return (xp, xs, jnp.stack(kp_l), jnp.stack(vp_l), jnp.stack(ap_l), jnp.stack(cp_l),
            jnp.stack(ks_l), jnp.stack(vs_l), jnp.stack(as_l), jnp.stack(cs_l))
```

```python
import functools
import math

import jax
import jax.numpy as jnp
from jax import lax
from jax.experimental import pallas as pl
from jax.experimental.pallas import tpu as pltpu

F32 = jnp.float32
BF16 = jnp.bfloat16

QK_DIM = 64
V_DIM = 2 * QK_DIM
ROT_DIM = QK_DIM // 4
ROPE_THETA = 500000.0
KA = 31
KC = 3
EPS = 1e-6
NEG_INF = -1e30
LANES = 128
HALO = 32
VMEM_LIMIT = 56 * 1024 * 1024


def _params(*sem):
    return pltpu.CompilerParams(dimension_semantics=sem, vmem_limit_bytes=VMEM_LIMIT)


def _const_spec(shape):
    nd = len(shape)
    return pl.BlockSpec(shape, lambda *_: (0,) * nd, pipeline_mode=pl.Buffered(1))


def _ada_kernel(c_ref, w_ref, b_ref, o_ref):
    c = c_ref[...]
    s = (c * jax.nn.sigmoid(c)).astype(BF16)
    o_ref[...] = jnp.dot(s, w_ref[...].astype(BF16), preferred_element_type=F32) + b_ref[...]


def _ada_mod(c_all, w_ada, b_ada):
    depth, d, n = w_ada.shape
    m = c_all.shape[0]
    tn = 1536
    assert n % tn == 0 and m % 8 == 0
    return pl.pallas_call(
        _ada_kernel,
        grid=(depth, n // tn),
        in_specs=[pl.BlockSpec((m, d), lambda l, j: (0, 0)),
                  pl.BlockSpec((None, d, tn), lambda l, j: (l, 0, j)),
                  pl.BlockSpec((None, 1, tn), lambda l, j: (l, 0, j))],
        out_specs=pl.BlockSpec((None, m, tn), lambda l, j: (l, 0, j)),
        out_shape=jax.ShapeDtypeStruct((depth, m, n), F32),
        compiler_params=_params("arbitrary", "arbitrary"),
        name="ada_mod",
    )(c_all, w_ada, b_ada.reshape(depth, 1, n))


def _rope_kernel(ang_ref, cos_ref, sa_ref, sb_ref):
    ang = ang_ref[...]
    d = lax.broadcasted_iota(jnp.int32, ang.shape, 1) % QK_DIM
    half = ROT_DIM // 2
    c, s = jnp.cos(ang), jnp.sin(ang)
    cos_ref[...] = jnp.where(d < ROT_DIM, c, 1.0)
    sa_ref[...] = jnp.where((d >= half) & (d < ROT_DIM), s, 0.0)
    sb_ref[...] = jnp.where(d < half, -s, 0.0)


def _rope_tables(pos):
    n = pos.shape[0]
    half = ROT_DIM // 2
    inv_freq = ROPE_THETA ** (-jnp.arange(half, dtype=F32) / half)
    lane_freq = inv_freq[jnp.arange(LANES) % half]
    ang = pos.astype(F32)[:, None] * lane_freq[None, :]
    tr = min(n, 1024)
    assert n % tr == 0 and tr % 8 == 0
    spec = pl.BlockSpec((tr, LANES), lambda i: (i, 0))
    return pl.pallas_call(
        _rope_kernel, grid=(n // tr,), in_specs=[spec], out_specs=[spec] * 3,
        out_shape=[jax.ShapeDtypeStruct((n, LANES), F32)] * 3,
        compiler_params=_params("arbitrary"), name="rope_tables",
    )(ang)


def _inproj_kernel(x_ref, mod_ref, n1w_ref, win_ref, gmat_ref, qkw_ref, cos_ref, sa_ref, sb_ref,
                   agb_ref, q_ref, k_ref, v_ref, kb_ref, vb_ref, *, aw, qc, vc, cw):
    x = x_ref[...]
    ms = jnp.mean(x * x, axis=-1, keepdims=True)
    u = x * lax.rsqrt(ms + EPS) * n1w_ref[...]
    u = u * (1.0 + mod_ref[1]) + mod_ref[0]
    z = jnp.dot(u.astype(BF16), win_ref[...], preferred_element_type=F32)

    o_q, o_k, o_v = 2 * aw, 2 * aw + qc, 2 * aw + 2 * qc
    o_gb = o_v + vc
    o_gc, o_gh = o_gb + cw, o_gb + 2 * cw

    agb_ref[:, 0:aw] = z[:, 0:aw] * jax.nn.sigmoid(z[:, aw:2 * aw])
    agb_ref[:, aw:aw + cw] = z[:, o_gc:o_gc + cw] * z[:, o_gh:o_gh + cw]
    agb_ref[:, aw + cw:aw + 2 * cw] = z[:, o_gb:o_gb + cw]

    v = z[:, o_v:o_v + vc]
    v_ref[...] = v
    vb_ref[...] = v.astype(BF16)

    cos, sa, sb = cos_ref[...], sa_ref[...], sb_ref[...]
    half = ROT_DIM // 2
    gm = gmat_ref[...]
    gw = gm.shape[0]
    for c0 in range(0, 2 * qc, gw):
        y = z[:, o_q + c0:o_q + c0 + gw]
        gms = jnp.dot((y * y).astype(BF16), gm, preferred_element_type=F32)
        y = y * lax.rsqrt(gms + EPS) * qkw_ref[:, c0:c0 + gw]
        for c1 in range(0, gw, LANES):
            yc = y[:, c1:c1 + LANES]
            r = (yc * cos + pltpu.roll(yc, half, 1) * sa + pltpu.roll(yc, LANES - half, 1) * sb)
            col = c0 + c1
            if col < qc:
                q_ref[:, col:col + LANES] = (r * (QK_DIM ** -0.5)).astype(BF16)
            else:
                k_ref[:, col - qc:col - qc + LANES] = r
                kb_ref[:, col - qc:col - qc + LANES] = r.astype(BF16)


def _inproj(x, mod, n1w, win_b, gmat, qkw, tables, *, tm, tiles_per_seq, table_tiles):
    n, d = x.shape
    aw, qc = d // 4, d // 2
    vc, cw = d // 2, d - d // 4 - d // 2
    ncols = win_b.shape[1]
    r = mod.shape[2]
    row = lambda w: pl.BlockSpec((tm, w), lambda i: (i, 0))
    tab = pl.BlockSpec((tm, LANES), lambda i: (i % table_tiles, 0))
    kern = functools.partial(_inproj_kernel, aw=aw, qc=qc, vc=vc, cw=cw)
    return pl.pallas_call(
        kern,
        grid=(n // tm,),
        in_specs=[row(d),
                  pl.BlockSpec((None, 6, r, d), lambda i: (i // tiles_per_seq, 0, 0, 0)),
                  _const_spec((1, d)), _const_spec((d, ncols)), _const_spec(gmat.shape),
                  _const_spec((1, 2 * qc)), tab, tab, tab],
        out_specs=[row(aw + 2 * cw), row(qc), row(qc), row(vc), row(qc), row(vc)],
        out_shape=[jax.ShapeDtypeStruct((n, aw + 2 * cw), F32),
                   jax.ShapeDtypeStruct((n, qc), BF16),
                   jax.ShapeDtypeStruct((n, qc), F32),
                   jax.ShapeDtypeStruct((n, vc), F32),
                   jax.ShapeDtypeStruct((n, qc), BF16),
                   jax.ShapeDtypeStruct((n, vc), BF16)],
        compiler_params=_params("arbitrary"),
        name="inproj",
    )(x, mod, n1w, win_b, gmat, qkw, *tables)


def _lambda(lp_ref, lam_init):
    s1 = jnp.sum(lp_ref[0:1, :] * lp_ref[1:2, :], axis=-1, keepdims=True)
    s2 = jnp.sum(lp_ref[2:3, :] * lp_ref[3:4, :], axis=-1, keepdims=True)
    return jnp.exp(s1) - jnp.exp(s2) + lam_init


def _subln(o, w, lam_init):
    ms = jnp.mean(o * o, axis=-1, keepdims=True)
    return o * lax.rsqrt(ms + EPS) * w * (1.0 - lam_init)


def _online_softmax_step(s, v, m_ref, l_ref, acc_ref):
    m_prev = m_ref[...]
    m_new = jnp.maximum(m_prev, jnp.max(s, axis=-1, keepdims=True))
    alpha = jnp.exp(m_prev - m_new)
    p = jnp.exp(s - m_new)
    l_ref[...] = alpha * l_ref[...] + jnp.sum(p, axis=-1, keepdims=True)
    acc_ref[...] = alpha * acc_ref[...] + jnp.dot(p.astype(BF16), v, preferred_element_type=F32)
    m_ref[...] = m_new


_NT = (((1,), (1,)), ((), ()))


def _flash_kernel(qi_ref, kj_ref, q_ref, k_ref, v_ref, lp_ref, sw_ref, o_ref,
                  m1, l1, a1, m2, l2, a2, *, lam_init):
    p = pl.program_id(2)
    i, j = qi_ref[p], kj_ref[p]

    @pl.when(j == 0)
    def _():
        for m, l, a in ((m1, l1, a1), (m2, l2, a2)):
            m[...] = jnp.full_like(m, NEG_INF)
            l[...] = jnp.zeros_like(l)
            a[...] = jnp.zeros_like(a)

    def step(masked):
        q = q_ref[...]
        k = k_ref[...]
        v = v_ref[...]
        lane = lax.broadcasted_iota(jnp.int32, q.shape, 1)
        zero = jnp.zeros_like(q)
        for comp, (m, l, a) in enumerate(((m1, l1, a1), (m2, l2, a2))):
            sel = (lane < QK_DIM) if comp == 0 else (lane >= QK_DIM)
            s = lax.dot_general(jnp.where(sel, q, zero), k, _NT, preferred_element_type=F32)
            if masked:
                r = lax.broadcasted_iota(jnp.int32, s.shape, 0)
                c = lax.broadcasted_iota(jnp.int32, s.shape, 1)
                s = jnp.where(c <= r, s, NEG_INF)
            _online_softmax_step(s, v, m, l, a)

    @pl.when(j < i)
    def _():
        step(False)

    @pl.when(j == i)
    def _():
        step(True)
        lam = _lambda(lp_ref, lam_init)
        o = a1[...] / l1[...] - lam * (a2[...] / l2[...])
        o_ref[...] = _subln(o, sw_ref[...], lam_init).astype(o_ref.dtype)


def _flash(qb, kb, vb, lp, sw, *, batch, seq, tq, lam_init):
    n, hc = qb.shape
    heads = hc // LANES
    nq = seq // tq
    pairs = [(i, j) for i in range(nq) for j in range(i + 1)]
    qi = jnp.asarray([p[0] for p in pairs], jnp.int32)
    kj = jnp.asarray([p[1] for p in pairs], jnp.int32)
    qspec = pl.BlockSpec((tq, LANES), lambda b, h, p, qi, kj: (b * nq + qi[p], h))
    kspec = pl.BlockSpec((tq, LANES), lambda b, h, p, qi, kj: (b * nq + kj[p], h))
    cspec = lambda shape: pl.BlockSpec(shape, lambda b, h, p, qi, kj: (0, 0))
    col = pltpu.VMEM((tq, 1), F32)
    acc = pltpu.VMEM((tq, LANES), F32)
    return pl.pallas_call(
        functools.partial(_flash_kernel, lam_init=lam_init),
        grid_spec=pltpu.PrefetchScalarGridSpec(
            num_scalar_prefetch=2,
            grid=(batch, heads, len(pairs)),
            in_specs=[qspec, kspec, kspec, cspec((8, LANES)), cspec((1, LANES))],
            out_specs=qspec,
            scratch_shapes=[col, col, acc, col, col, acc]),
        out_shape=jax.ShapeDtypeStruct((n, hc), BF16),
        compiler_params=_params("arbitrary", "arbitrary", "arbitrary"),
        name="flash_diff_attn",
    )(qi, kj, qb, kb, vb, lp, sw)


def _paged_kernel(pt_ref, q_ref, kn_ref, vn_ref, lp_ref, sw_ref, *rest,
                  pages, n_new, lam_init, heads):
    k_refs, v_refs = rest[:pages], rest[pages:2 * pages]
    o_ref, m, l, acc = rest[2 * pages:]
    c = pl.program_id(1)

    @pl.when(c == 0)
    def _():
        m[...] = jnp.full_like(m, NEG_INF)
        l[...] = jnp.zeros_like(l)
        acc[...] = jnp.zeros_like(acc)

    q = q_ref[...]
    kcat = jnp.concatenate([r[...].astype(BF16) for r in k_refs], axis=0)
    vcat = jnp.concatenate([r[...].astype(BF16) for r in v_refs], axis=0)
    s = lax.dot_general(q, kcat, _NT, preferred_element_type=F32)
    _online_softmax_step(s, vcat, m, l, acc)

    @pl.when(c == pl.num_programs(1) - 1)
    def _():
        s = lax.dot_general(q, kn_ref[...].astype(BF16), _NT, preferred_element_type=F32)
        t_q = lax.broadcasted_iota(jnp.int32, s.shape, 0) % 8
        t_k = lax.broadcasted_iota(jnp.int32, s.shape, 1)
        s = jnp.where((t_k <= t_q) & (t_k < n_new), s, NEG_INF)
        _online_softmax_step(s, vn_ref[...].astype(BF16), m, l, acc)
        lam = _lambda(lp_ref, lam_init)
        half = heads * 8
        for h in range(heads):
            r1, r2 = slice(h * 8, h * 8 + 8), slice(half + h * 8, half + h * 8 + 8)
            cols = slice(h * LANES, (h + 1) * LANES)
            o = acc[r1, cols] / l[r1, :] - lam * (acc[r2, cols] / l[r2, :])
            o_ref[:, cols] = _subln(o, sw_ref[...], lam_init)


def _paged(qblk, knew, vnew, cache_k, cache_v, page_table, lp, sw, *, layer, n_new, lam_init):
    db, rows, hc = qblk.shape
    heads = hc // LANES
    n_pages = page_table.shape[1]
    page = cache_k.shape[2]
    pages = math.gcd(n_pages, 8)
    chunks = n_pages // pages
    pt = page_table.reshape(-1).astype(jnp.int32)

    def page_spec(i):
        return pl.BlockSpec((None, None, page, hc),
                            lambda b, c, pt: (layer, pt[b * n_pages + c * pages + i], 0, 0))

    per_seq = lambda r: pl.BlockSpec((None, r, hc), lambda b, c, pt: (b, 0, 0))
    cspec = lambda shape: pl.BlockSpec(shape, lambda b, c, pt: (0, 0))
    kern = functools.partial(_paged_kernel, pages=pages, n_new=n_new, lam_init=lam_init, heads=heads)
    return pl.pallas_call(
        kern,
        grid_spec=pltpu.PrefetchScalarGridSpec(
            num_scalar_prefetch=1,
            grid=(db, chunks),
            in_specs=[per_seq(rows), per_seq(knew.shape[1]), per_seq(vnew.shape[1]),
                      cspec((8, LANES)), cspec((1, LANES))]
                     + [page_spec(i) for i in range(pages)] * 2,
            out_specs=per_seq(8),
            scratch_shapes=[pltpu.VMEM((rows, 1), F32), pltpu.VMEM((rows, 1), F32),
                            pltpu.VMEM((rows, hc), F32)]),
        out_shape=jax.ShapeDtypeStruct((db, 8, hc), F32),
        compiler_params=_params("arbitrary", "arbitrary"),
        name="paged_diff_attn",
    )(pt, qblk, knew, vnew, lp, sw, *([cache_k] * pages), *([cache_v] * pages))


def _mix_tail(x, ya_pre, yc_pre, gb, yb, mod_ref, cab_ref, lnw_ref, lnb_ref, wout_ref, n2w_ref,
              x1_ref, h_ref, *, aw, bw):
    ya = ya_pre + cab_ref[...]
    mu = jnp.mean(ya, axis=-1, keepdims=True)
    var = jnp.mean(jnp.square(ya - mu), axis=-1, keepdims=True)
    ya = (ya - mu) * lax.rsqrt(var + EPS) * lnw_ref[...] + lnb_ref[...]
    ya = ya * jax.nn.sigmoid(ya)
    yc = gb * yc_pre
    y = jnp.dot(ya.astype(BF16), wout_ref[0:aw, :], preferred_element_type=F32)
    y += jnp.dot(yb.astype(BF16), wout_ref[aw:aw + bw, :], preferred_element_type=F32)
    y += jnp.dot(yc.astype(BF16), wout_ref[aw + bw:, :], preferred_element_type=F32)
    x1 = x + mod_ref[2] * y
    x1_ref[...] = x1
    ms = jnp.mean(x1 * x1, axis=-1, keepdims=True)
    h = x1 * lax.rsqrt(ms + EPS) * n2w_ref[...]
    h_ref[...] = (h * (1.0 + mod_ref[4]) + mod_ref[3]).astype(h_ref.dtype)


def _mix_flat_kernel(x_ref, mod_ref, agb_ref, halo_ref, yb_ref, caw_ref, cab_ref, lnw_ref, lnb_ref,
                     ccw_ref, wout_ref, n2w_ref, x1_ref, h_ref, buf, *, aw, cw, bw, tiles_per_seq):
    tm = x_ref.shape[0]
    first = pl.program_id(0) % tiles_per_seq == 0
    halo = halo_ref[:, 0:aw + cw]
    buf[0:HALO, :] = jnp.where(first, jnp.zeros_like(halo), halo)
    buf[HALO:, :] = agb_ref[:, 0:aw + cw]
    ya = jnp.zeros((tm, aw), F32)
    for j in range(KA):
        off = HALO - (KA - 1) + j
        ya += caw_ref[j:j + 1, :] * buf[off:off + tm, 0:aw]
    yc = jnp.zeros((tm, cw), F32)
    for j in range(KC):
        off = HALO - (KC - 1) + j
        yc += ccw_ref[j:j + 1, :] * buf[off:off + tm, aw:aw + cw]
    gb = agb_ref[:, aw + cw:aw + 2 * cw]
    _mix_tail(x_ref[...], ya, yc, gb, yb_ref[...], mod_ref, cab_ref, lnw_ref, lnb_ref, wout_ref,
              n2w_ref, x1_ref, h_ref, aw=aw, bw=bw)


def _mix_hist_kernel(x_ref, mod_ref, af_ref, gf_ref, gb_ref, yb_ref, caw_ref, cab_ref, lnw_ref,
                     lnb_ref, ccw_ref, wout_ref, n2w_ref, x1_ref, h_ref, *, aw, cw, bw):
    ts = af_ref.shape[0] - (KA - 1)
    nb = af_ref.shape[1]
    ya = jnp.zeros((ts, nb, aw), F32)
    for j in range(KA):
        ya += caw_ref[j:j + 1, :][None] * af_ref[j:j + ts]
    yc = jnp.zeros((ts, nb, cw), F32)
    for j in range(KC):
        yc += ccw_ref[j:j + 1, :][None] * gf_ref[j:j + ts]
    _mix_tail(x_ref[...], ya.reshape(ts * nb, aw), yc.reshape(ts * nb, cw), gb_ref[...],
              yb_ref[...], mod_ref, cab_ref, lnw_ref, lnb_ref, wout_ref, n2w_ref, x1_ref, h_ref,
              aw=aw, bw=bw)


def _mix_weight_specs(lw, d, aw, cw):
    return [_const_spec((KA, aw)), _const_spec((1, aw)), _const_spec((1, aw)), _const_spec((1, aw)),
            _const_spec((KC, cw)), _const_spec((d, d)), _const_spec((1, d))]


def _mix_weights(lw):
    return (lw["conv_a_w"], lw["conv_a_b"], lw["ln_a_w"], lw["ln_a_b"], lw["conv_c_w"],
            lw["w_out_b"], lw["norm2_w"])


def _mix_flat(x, mod, agb, yb, lw, *, tm, tiles_per_seq):
    n, d = x.shape
    aw, bw = d // 4, d // 2
    cw = d - aw - bw
    row = lambda w: pl.BlockSpec((tm, w), lambda i: (i, 0))
    hpt = tm // HALO
    halo = pl.BlockSpec((HALO, aw + 2 * cw), lambda i: (jnp.maximum(i * hpt - 1, 0), 0))
    kern = functools.partial(_mix_flat_kernel, aw=aw, cw=cw, bw=bw, tiles_per_seq=tiles_per_seq)
    return pl.pallas_call(
        kern,
        grid=(n // tm,),
        in_specs=[row(d),
                  pl.BlockSpec((None, 6, 1, d), lambda i: (i // tiles_per_seq, 0, 0, 0)),
                  row(aw + 2 * cw), halo, row(bw)] + _mix_weight_specs(lw, d, aw, cw),
        out_specs=[row(d), row(d)],
        out_shape=[jax.ShapeDtypeStruct((n, d), F32), jax.ShapeDtypeStruct((n, d), BF16)],
        scratch_shapes=[pltpu.VMEM((HALO + tm, aw + cw), F32)],
        compiler_params=_params("arbitrary"),
        name="mix_flat",
    )(x, mod, agb, agb, yb, *_mix_weights(lw))


def _mix_hist(x, mod, af, gf, gb, yb, lw):
    n, d = x.shape
    aw, bw = d // 4, d // 2
    cw = d - aw - bw
    full = lambda a: pl.BlockSpec(a.shape, lambda i: (0,) * a.ndim)
    kern = functools.partial(_mix_hist_kernel, aw=aw, cw=cw, bw=bw)
    return pl.pallas_call(
        kern,
        grid=(1,),
        in_specs=[full(x), pl.BlockSpec((None,) + mod.shape[1:], lambda i: (0, 0, 0, 0)),
                  full(af), full(gf), full(gb), full(yb)] + _mix_weight_specs(lw, d, aw, cw),
        out_specs=[pl.BlockSpec((n, d), lambda i: (0, 0))] * 2,
        out_shape=[jax.ShapeDtypeStruct((n, d), F32), jax.ShapeDtypeStruct((n, d), BF16)],
        compiler_params=_params("arbitrary"),
        name="mix_hist",
    )(x, mod, af, gf, gb, yb, *_mix_weights(lw))


def _mlp_kernel(x_ref, h_ref, mod_ref, wup_ref, wdn_ref, o_ref, *, fc):
    h = h_ref[...]
    f = wup_ref.shape[1]
    y = jnp.zeros(x_ref.shape, F32)
    for c0 in range(0, f, fc):
        hid = jnp.dot(h, wup_ref[:, c0:c0 + fc], preferred_element_type=F32)
        hid = jnp.square(jnp.maximum(hid, 0.0)).astype(BF16)
        y += jnp.dot(hid, wdn_ref[c0:c0 + fc, :], preferred_element_type=F32)
    o_ref[...] = x_ref[...] + mod_ref[5] * y


def _mlp(x1, h, mod, wup_b, wdn_b, *, tm, tiles_per_seq):
    n, d = x1.shape
    f = wup_b.shape[1]
    r = mod.shape[2]
    row = pl.BlockSpec((tm, d), lambda i: (i, 0))
    return pl.pallas_call(
        functools.partial(_mlp_kernel, fc=min(f, 1024)),
        grid=(n // tm,),
        in_specs=[row, row,
                  pl.BlockSpec((None, 6, r, d), lambda i: (i // tiles_per_seq, 0, 0, 0)),
                  _const_spec((d, f)), _const_spec((f, d))],
        out_specs=row,
        out_shape=jax.ShapeDtypeStruct((n, d), F32),
        compiler_params=_params("arbitrary"),
        name="mlp",
    )(x1, h, mod, wup_b, wdn_b)


def _group_matrix(width):
    g = jnp.arange(width) // QK_DIM
    return jnp.where(g[:, None] == g[None, :], 1.0 / QK_DIM, 0.0).astype(BF16)


def kernel(x_prompt, x_sample, cache_k, cache_v, state_conv_a, state_conv_c, page_table, c_prompt, c_sample, norm1_w, norm2_w, w_ada, b_ada, w_in, conv_a_w, conv_a_b, ln_a_w, ln_a_b, q_norm_w, k_norm_w, lambda_q1, lambda_k1, lambda_q2, lambda_k2, subln_w, conv_c_w, w_out, w_mlp_up, w_mlp_down):
    depth = w_in.shape[0]
    bp, seq, d = x_prompt.shape
    db, ts, _ = x_sample.shape
    aw, bw = d // 4, d // 2
    cw = d - aw - bw
    heads = bw // V_DIM
    hc = heads * V_DIM
    n_pages, page = page_table.shape[1], cache_k.shape[2]
    past = n_pages * page
    tm = min(seq, 512)
    assert seq % tm == 0 and tm % HALO == 0 and (ts * db) % 8 == 0 and db % 8 == 0
    tiles = seq // tm

    n_c = bp + db
    n_cp = -(-n_c // 8) * 8
    c_all = jnp.zeros((n_cp, d), F32).at[:n_c].set(jnp.concatenate([c_prompt, c_sample], 0))
    mod_all = _ada_mod(c_all, w_ada, b_ada).reshape(depth, n_cp, 6, d)
    mod_p = mod_all[:, :bp, :, None, :]
    mod_s = jnp.tile(mod_all[:, bp:n_c], (1, ts, 1, 1)).transpose(0, 2, 1, 3)[:, None]

    tab_p = _rope_tables(jnp.arange(seq))
    pos_s = jnp.repeat(past + jnp.arange(ts), db)
    tab_s = _rope_tables(pos_s)

    gmat = _group_matrix(2 * LANES)
    cache_k2 = cache_k.reshape(depth, cache_k.shape[1], page, hc)
    cache_v2 = cache_v.reshape(depth, cache_v.shape[1], page, hc)

    lane = jnp.arange(hc)
    sel = (lane[None, None, :] // QK_DIM
           == (jnp.arange(heads)[None, :, None] * 2 + jnp.arange(2)[:, None, None])).astype(BF16)

    xp = x_prompt.reshape(bp * seq, d)
    xs = x_sample.transpose(1, 0, 2).reshape(ts * db, d)
    outs = {k: [] for k in ("kp", "vp", "ap", "cp", "ks", "vs", "as", "cs")}
    for l in range(depth):
        lam_init = 0.8 - 0.6 * math.exp(-0.3 * l)
        lw = {"conv_a_w": conv_a_w[l], "conv_a_b": conv_a_b[l][None], "ln_a_w": ln_a_w[l][None],
              "ln_a_b": ln_a_b[l][None], "conv_c_w": conv_c_w[l], "w_out_b": w_out[l].astype(BF16),
              "norm2_w": norm2_w[l][None]}
        win_b = w_in[l].astype(BF16)
        wup_b, wdn_b = w_mlp_up[l].astype(BF16), w_mlp_down[l].astype(BF16)
        n1w = norm1_w[l][None]
        qkw = jnp.concatenate([jnp.tile(q_norm_w[l], 2 * heads), jnp.tile(k_norm_w[l], 2 * heads)])[None]
        lp = jnp.zeros((8, LANES), F32).at[:4, :QK_DIM].set(
            jnp.stack([lambda_q1[l], lambda_k1[l], lambda_q2[l], lambda_k2[l]]))
        sw = subln_w[l][None]

        agb, qb, k, v, kb, vb = _inproj(xp, mod_p[l], n1w, win_b, gmat, qkw, tab_p,
                                        tm=tm, tiles_per_seq=tiles, table_tiles=tiles)
        yb = _flash(qb, kb, vb, lp, sw, batch=bp, seq=seq, tq=tm, lam_init=lam_init)
        x1, h = _mix_flat(xp, mod_p[l], agb, yb, lw, tm=tm, tiles_per_seq=tiles)
        xp = _mlp(x1, h, mod_p[l], wup_b, wdn_b, tm=tm, tiles_per_seq=tiles)
        outs["kp"].append(k.reshape(bp, seq, heads, 2, QK_DIM))
        outs["vp"].append(v.reshape(bp, seq, heads, V_DIM))
        agb3 = agb.reshape(bp, seq, aw + 2 * cw)
        if seq >= KA - 1:
            outs["ap"].append(agb3[:, seq - (KA - 1):, :aw])
            outs["cp"].append(agb3[:, seq - (KC - 1):, aw:aw + cw])
        else:
            za = jnp.zeros((bp, KA - 1 - seq, aw), F32)
            zc = jnp.zeros((bp, max(KC - 1 - seq, 0), cw), F32)
            outs["ap"].append(jnp.concatenate([za, agb3[..., :aw]], 1))
            outs["cp"].append(jnp.concatenate([zc, agb3[..., aw:aw + cw]], 1)[:, -(KC - 1):])

        n_s = ts * db
        agb, qb, k, v, kb, vb = _inproj(xs, mod_s[l], n1w, win_b, gmat, qkw, tab_s,
                                        tm=n_s, tiles_per_seq=1, table_tiles=1)
        k_seq = k.reshape(ts, db, hc).transpose(1, 0, 2)
        v_seq = v.reshape(ts, db, hc).transpose(1, 0, 2)
        q_seq = qb.reshape(ts, db, hc).transpose(1, 0, 2)
        q_pad = jnp.zeros((db, 8, hc), BF16).at[:, :ts].set(q_seq)
        qblk = (q_pad[:, None, None] * sel[None, :, :, None, :]).reshape(db, 2 * heads * 8, hc)
        knew = jnp.zeros((db, LANES, hc), F32).at[:, :ts].set(k_seq)
        vnew = jnp.zeros((db, LANES, hc), F32).at[:, :ts].set(v_seq)
        o = _paged(qblk, knew, vnew, cache_k2, cache_v2, page_table, lp, sw,
                   layer=l, n_new=ts, lam_init=lam_init)
        yb = o[:, :ts].transpose(1, 0, 2).reshape(n_s, hc)
        af = jnp.concatenate([state_conv_a[l].transpose(1, 0, 2), agb[:, :aw].reshape(ts, db, aw)], 0)
        gf = jnp.concatenate([state_conv_c[l].transpose(1, 0, 2),
                              agb[:, aw:aw + cw].reshape(ts, db, cw)], 0)
        x1, h = _mix_hist(xs, mod_s[l], af, gf, agb[:, aw + cw:], yb, lw)
        xs = _mlp(x1, h, mod_s[l], wup_b, wdn_b, tm=n_s, tiles_per_seq=1)
        outs["ks"].append(k_seq.reshape(db, ts, heads, 2, QK_DIM))
        outs["vs"].append(v_seq.reshape(db, ts, heads, V_DIM))
        outs["as"].append(af[-(KA - 1):].transpose(1, 0, 2))
        outs["cs"].append(gf[-(KC - 1):].transpose(1, 0, 2))

    st = lambda key: jnp.stack(outs[key])
    return (xp.reshape(bp, seq, d), xs.reshape(ts, db, d).transpose(1, 0, 2),
            st("kp"), st("vp"), st("ap"), st("cp"), st("ks"), st("vs"), st("as"), st("cs"))
```

```python
import functools
import math

import jax
import jax.numpy as jnp
from jax import lax
from jax.experimental import pallas as pl
from jax.experimental.pallas import tpu as pltpu

F32 = jnp.float32
BF16 = jnp.bfloat16

QK_DIM = 64
V_DIM = 2 * QK_DIM
ROT_DIM = QK_DIM // 4
ROPE_THETA = 500000.0
KA = 31
KC = 3
EPS = 1e-6
NEG_INF = -1e30
Q_SCALE = QK_DIM ** -0.5 * math.log2(math.e)
LANES = 128
HALO = 32
VMEM_LIMIT = 56 * 1024 * 1024


def _params(*sem):
    return pltpu.CompilerParams(dimension_semantics=sem, vmem_limit_bytes=VMEM_LIMIT)


def _const_spec(shape):
    nd = len(shape)
    return pl.BlockSpec(shape, lambda *_: (0,) * nd, pipeline_mode=pl.Buffered(1))


def _ada_kernel(c_ref, w_ref, b_ref, o_ref):
    c = c_ref[...]
    s = (c * jax.nn.sigmoid(c)).astype(BF16)
    o_ref[...] = jnp.dot(s, w_ref[...].astype(BF16), preferred_element_type=F32) + b_ref[...]


def _ada_mod(c_all, w_ada, b_ada):
    depth, d, n = w_ada.shape
    m = c_all.shape[0]
    tn = 1536
    assert n % tn == 0 and m % 8 == 0
    return pl.pallas_call(
        _ada_kernel,
        grid=(depth, n // tn),
        in_specs=[pl.BlockSpec((m, d), lambda l, j: (0, 0)),
                  pl.BlockSpec((None, d, tn), lambda l, j: (l, 0, j)),
                  pl.BlockSpec((None, 1, tn), lambda l, j: (l, 0, j))],
        out_specs=pl.BlockSpec((None, m, tn), lambda l, j: (l, 0, j)),
        out_shape=jax.ShapeDtypeStruct((depth, m, n), F32),
        compiler_params=_params("arbitrary", "arbitrary"),
        name="ada_mod",
    )(c_all, w_ada, b_ada.reshape(depth, 1, n))


def _rope_kernel(ang_ref, cos_ref, sa_ref, sb_ref):
    ang = ang_ref[...]
    d = lax.broadcasted_iota(jnp.int32, ang.shape, 1) % QK_DIM
    half = ROT_DIM // 2
    c, s = jnp.cos(ang), jnp.sin(ang)
    cos_ref[...] = jnp.where(d < ROT_DIM, c, 1.0)
    sa_ref[...] = jnp.where((d >= half) & (d < ROT_DIM), s, 0.0)
    sb_ref[...] = jnp.where(d < half, -s, 0.0)


def _rope_tables(pos):
    n = pos.shape[0]
    half = ROT_DIM // 2
    inv_freq = ROPE_THETA ** (-jnp.arange(half, dtype=F32) / half)
    lane_freq = inv_freq[jnp.arange(LANES) % half]
    ang = pos.astype(F32)[:, None] * lane_freq[None, :]
    tr = min(n, 1024)
    assert n % tr == 0 and tr % 8 == 0
    spec = pl.BlockSpec((tr, LANES), lambda i: (i, 0))
    return pl.pallas_call(
        _rope_kernel, grid=(n // tr,), in_specs=[spec], out_specs=[spec] * 3,
        out_shape=[jax.ShapeDtypeStruct((n, LANES), F32)] * 3,
        compiler_params=_params("arbitrary"), name="rope_tables",
    )(ang)


def _inproj_kernel(x_ref, mod_ref, n1w_ref, win_ref, gmat_ref, qkw_ref, cos_ref, sa_ref, sb_ref,
                   agb_ref, qt_ref, kt_ref, kb_ref, v_ref, vt_ref, *, aw, qc, vc, cw):
    x = x_ref[...]
    ms = jnp.mean(x * x, axis=-1, keepdims=True)
    u = x * lax.rsqrt(ms + EPS) * n1w_ref[...]
    u = u * (1.0 + mod_ref[1]) + mod_ref[0]
    z = jnp.dot(u.astype(BF16), win_ref[...], preferred_element_type=F32)

    o_q, o_k, o_v = 2 * aw, 2 * aw + qc, 2 * aw + 2 * qc
    o_gb = o_v + vc
    o_gc, o_gh = o_gb + cw, o_gb + 2 * cw

    agb_ref[:, 0:aw] = z[:, 0:aw] * jax.nn.sigmoid(z[:, aw:2 * aw])
    agb_ref[:, aw:aw + cw] = z[:, o_gc:o_gc + cw] * z[:, o_gh:o_gh + cw]
    agb_ref[:, aw + cw:aw + 2 * cw] = z[:, o_gb:o_gb + cw]

    v = z[:, o_v:o_v + vc]
    v_ref[...] = v
    for c1 in range(0, vc, LANES):
        vt_ref[c1:c1 + LANES, :] = v[:, c1:c1 + LANES].T.astype(BF16)

    cos, sa, sb = cos_ref[...], sa_ref[...], sb_ref[...]
    half = ROT_DIM // 2
    gm = gmat_ref[...]
    gw = gm.shape[0]
    for c0 in range(0, 2 * qc, gw):
        y = z[:, o_q + c0:o_q + c0 + gw]
        gms = jnp.dot((y * y).astype(BF16), gm, preferred_element_type=F32)
        y = y * lax.rsqrt(gms + EPS) * qkw_ref[:, c0:c0 + gw]
        for c1 in range(0, gw, LANES):
            yc = y[:, c1:c1 + LANES]
            r = (yc * cos + pltpu.roll(yc, half, 1) * sa + pltpu.roll(yc, LANES - half, 1) * sb)
            col = c0 + c1
            if col < qc:
                qt_ref[col:col + LANES, :] = (r * Q_SCALE).T.astype(BF16)
            else:
                kt_ref[col - qc:col - qc + LANES, :] = r.T
                kb_ref[:, col - qc:col - qc + LANES] = r.astype(BF16)


def _inproj(x, mod, n1w, win_b, gmat, qkw, tables, *, tm, tiles_per_seq, table_tiles):
    n, d = x.shape
    aw, qc = d // 4, d // 2
    vc, cw = d // 2, d - d // 4 - d // 2
    ncols = win_b.shape[1]
    r = mod.shape[2]
    row = lambda w: pl.BlockSpec((tm, w), lambda i: (i, 0))
    tab = pl.BlockSpec((tm, LANES), lambda i: (i % table_tiles, 0))
    tile_t = lambda w: pl.BlockSpec((None, w, tm), lambda i: (i, 0, 0))
    kern = functools.partial(_inproj_kernel, aw=aw, qc=qc, vc=vc, cw=cw)
    return pl.pallas_call(
        kern,
        grid=(n // tm,),
        in_specs=[row(d),
                  pl.BlockSpec((None, 6, r, d), lambda i: (i // tiles_per_seq, 0, 0, 0)),
                  _const_spec((1, d)), _const_spec((d, ncols)), _const_spec(gmat.shape),
                  _const_spec((1, 2 * qc)), tab, tab, tab],
        out_specs=[row(aw + 2 * cw), tile_t(qc),
                   pl.BlockSpec((None, qc, tm), lambda i: (i // tiles_per_seq, 0, i % tiles_per_seq)),
                   row(qc), row(vc), tile_t(vc)],
        out_shape=[jax.ShapeDtypeStruct((n, aw + 2 * cw), F32),
                   jax.ShapeDtypeStruct((n // tm, qc, tm), BF16),
                   jax.ShapeDtypeStruct((n // (tm * tiles_per_seq), qc, tm * tiles_per_seq), F32),
                   jax.ShapeDtypeStruct((n, qc), BF16),
                   jax.ShapeDtypeStruct((n, vc), F32),
                   jax.ShapeDtypeStruct((n // tm, vc, tm), BF16)],
        compiler_params=_params("arbitrary"),
        name="inproj",
    )(x, mod, n1w, win_b, gmat, qkw, *tables)


def _lambda(lp_ref, lam_init):
    s1 = jnp.sum(lp_ref[0:1, :] * lp_ref[1:2, :], axis=-1, keepdims=True)
    s2 = jnp.sum(lp_ref[2:3, :] * lp_ref[3:4, :], axis=-1, keepdims=True)
    return jnp.exp(s1) - jnp.exp(s2) + lam_init


def _subln(o, w, lam_init):
    ms = jnp.mean(o * o, axis=-1, keepdims=True)
    return o * lax.rsqrt(ms + EPS) * w * (1.0 - lam_init)


_NT = (((1,), (1,)), ((), ()))


def _flash_kernel(qt_ref, k_ref, vt_ref, lp_ref, sw_ref, o_ref, qc_ref, sa_ref, sb_ref, xa_ref, xb_ref,
                  m_ref, l_ref, acc_ref, *, lam_init):
    i = pl.program_id(2)
    tk = k_ref.shape[0] // vt_ref.shape[0]
    qt = qt_ref[...]
    row = lax.broadcasted_iota(jnp.int32, qt.shape, 0)
    zero = jnp.zeros_like(qt)
    qc_ref[0] = jnp.where(row < QK_DIM, qt, zero)
    qc_ref[1] = jnp.where(row >= QK_DIM, qt, zero)
    m_ref[...] = jnp.full_like(m_ref, NEG_INF)
    l_ref[...] = jnp.zeros_like(l_ref)
    acc_ref[...] = jnp.zeros_like(acc_ref)

    def produce(j, s_ref, x_ref):
        k = k_ref[pl.ds(pl.multiple_of(j * tk, tk), tk), :]
        for c in range(2):
            s = jnp.dot(k, qc_ref[c], preferred_element_type=F32)
            s_ref[c] = s
            x_ref[c] = jnp.max(s, axis=0, keepdims=True)

    def consume(j, s_ref, x_ref, diag):
        vt = vt_ref[j]
        for c in range(2):
            s = s_ref[c]
            if diag:
                kr = lax.broadcasted_iota(jnp.int32, s.shape, 0)
                qq = lax.broadcasted_iota(jnp.int32, s.shape, 1)
                s = jnp.where(kr <= qq, s, NEG_INF)
                x = jnp.max(s, axis=0, keepdims=True)
            else:
                x = x_ref[c]
            m_prev = m_ref[c]
            m_new = jnp.maximum(m_prev, x)
            alpha = jnp.exp2(m_prev - m_new)
            p = jnp.exp2(s - m_new)
            l_ref[c] = alpha * l_ref[c] + jnp.sum(p, axis=0, keepdims=True)
            acc_ref[c] = alpha * acc_ref[c] + jnp.dot(vt, p.astype(BF16), preferred_element_type=F32)
            m_ref[c] = m_new

    produce(0, sa_ref, xa_ref)

    def pair(t, carry):
        j = 2 * t
        produce(j + 1, sb_ref, xb_ref)
        consume(j, sa_ref, xa_ref, False)
        produce(j + 2, sa_ref, xa_ref)
        consume(j + 1, sb_ref, xb_ref, False)
        return carry

    lax.fori_loop(0, i // 2, pair, 0)

    @pl.when(i % 2 == 0)
    def _():
        consume(i, sa_ref, xa_ref, True)

    @pl.when(i % 2 == 1)
    def _():
        produce(i, sb_ref, xb_ref)
        consume(i - 1, sa_ref, xa_ref, False)
        consume(i, sb_ref, xb_ref, True)

    lam = _lambda(lp_ref, lam_init)
    ot = acc_ref[0] / l_ref[0] - lam * (acc_ref[1] / l_ref[1])
    o_ref[...] = _subln(ot.T, sw_ref[...], lam_init).astype(o_ref.dtype)


def _flash(qt, kb, vt, lp, sw, *, batch, seq, lam_init):
    n, hc = kb.shape
    heads = hc // LANES
    tq, tk = qt.shape[2], vt.shape[2]
    assert tq == tk
    nq, nk = seq // tq, seq // tk
    cspec = lambda shape: pl.BlockSpec(shape, lambda b, h, i: (0, 0))
    stat = pltpu.VMEM((2, 1, tq), F32)
    score = pltpu.VMEM((2, tk, tq), F32)
    return pl.pallas_call(
        functools.partial(_flash_kernel, lam_init=lam_init),
        grid=(batch, heads, nq),
        in_specs=[pl.BlockSpec((None, LANES, tq), lambda b, h, i: (b * nq + i, h, 0)),
                  pl.BlockSpec((seq, LANES), lambda b, h, i: (b, h)),
                  pl.BlockSpec((nk, LANES, tk), lambda b, h, i: (b, h, 0)),
                  cspec((8, LANES)), cspec((1, LANES))],
        out_specs=pl.BlockSpec((tq, LANES), lambda b, h, i: (b * nq + i, h)),
        scratch_shapes=[pltpu.VMEM((2, LANES, tq), BF16), score, score, stat, stat, stat, stat,
                        pltpu.VMEM((2, V_DIM, tq), F32)],
        out_shape=jax.ShapeDtypeStruct((n, hc), BF16),
        compiler_params=_params("arbitrary", "arbitrary", "arbitrary"),
        name="flash_diff_attn",
    )(qt, kb, vt, lp, sw)


def _paged_kernel(pt_ref, q_ref, kn_ref, vn_ref, lp_ref, sw_ref, *rest,
                  pages, n_new, lam_init, heads):
    k_refs, v_refs = rest[:pages], rest[pages:2 * pages]
    o_ref, m, l, acc = rest[2 * pages:]
    c = pl.program_id(1)
    page = k_refs[0].shape[1]
    rows_h = 2 * 8

    @pl.when(c == 0)
    def _():
        m[...] = jnp.full_like(m, NEG_INF)
        l[...] = jnp.zeros_like(l)
        acc[...] = jnp.zeros_like(acc)

    def update(s, v_of_head):
        m_prev = m[...]
        m_new = jnp.maximum(m_prev, jnp.max(s, axis=-1, keepdims=True))
        alpha = jnp.exp2(m_prev - m_new)
        p = jnp.exp2(s - m_new)
        l[...] = alpha * l[...] + jnp.sum(p, axis=-1, keepdims=True)
        m[...] = m_new
        pb = p.astype(BF16)
        for h in range(heads):
            r = slice(h * rows_h, (h + 1) * rows_h)
            acc[r, :] = alpha[r, :] * acc[r, :] + jnp.dot(pb[r, :], v_of_head(h),
                                                          preferred_element_type=F32)

    q = q_ref[...]
    kcat = jnp.concatenate([r[...].astype(BF16) for r in k_refs], axis=1)
    s = jnp.dot(q, kcat, preferred_element_type=F32)
    update(s, lambda h: jnp.concatenate(
        [r[pl.ds(h, page, stride=heads), :].astype(BF16) for r in v_refs], axis=0))

    @pl.when(c == pl.num_programs(1) - 1)
    def _():
        s = lax.dot_general(q, kn_ref[...].astype(BF16), _NT, preferred_element_type=F32)
        t_q = lax.broadcasted_iota(jnp.int32, s.shape, 0) % 8
        t_k = lax.broadcasted_iota(jnp.int32, s.shape, 1)
        s = jnp.where((t_k <= t_q) & (t_k < n_new), s, NEG_INF)
        update(s, lambda h: vn_ref[:, h * LANES:(h + 1) * LANES].astype(BF16))
        lam = _lambda(lp_ref, lam_init)
        for h in range(heads):
            r1 = slice(h * rows_h, h * rows_h + 8)
            r2 = slice(h * rows_h + 8, (h + 1) * rows_h)
            o = acc[r1, :] / l[r1, :] - lam * (acc[r2, :] / l[r2, :])
            o_ref[:, h * LANES:(h + 1) * LANES] = _subln(o, sw_ref[...], lam_init)


def _paged(qblk, knew, vnew, kt_pages, v_pages, page_table, lp, sw, *, layer, n_new, lam_init):
    db, rows, hc = qblk.shape
    heads = hc // LANES
    n_pages = page_table.shape[1]
    page = kt_pages.shape[3]
    pages = math.gcd(n_pages, 8)
    chunks = n_pages // pages
    pt = page_table.reshape(-1).astype(jnp.int32)

    def page_spec(i, shape):
        return pl.BlockSpec((None, None) + shape,
                            lambda b, c, pt: (layer, pt[b * n_pages + c * pages + i], 0, 0))

    per_seq = lambda r: pl.BlockSpec((None, r, hc), lambda b, c, pt: (b, 0, 0))
    cspec = lambda shape: pl.BlockSpec(shape, lambda b, c, pt: (0, 0))
    kern = functools.partial(_paged_kernel, pages=pages, n_new=n_new, lam_init=lam_init, heads=heads)
    return pl.pallas_call(
        kern,
        grid_spec=pltpu.PrefetchScalarGridSpec(
            num_scalar_prefetch=1,
            grid=(db, chunks),
            in_specs=[per_seq(rows), per_seq(knew.shape[1]), per_seq(vnew.shape[1]),
                      cspec((8, LANES)), cspec((1, LANES))]
                     + [page_spec(i, (hc, page)) for i in range(pages)]
                     + [page_spec(i, (page * heads, LANES)) for i in range(pages)],
            out_specs=per_seq(8),
            scratch_shapes=[pltpu.VMEM((rows, 1), F32), pltpu.VMEM((rows, 1), F32),
                            pltpu.VMEM((rows, LANES), F32)]),
        out_shape=jax.ShapeDtypeStruct((db, 8, hc), F32),
        compiler_params=_params("arbitrary", "arbitrary"),
        name="paged_diff_attn",
    )(pt, qblk, knew, vnew, lp, sw, *([kt_pages] * pages), *([v_pages] * pages))


def _mix_tail(x, ya_pre, yc_pre, gb, yb, mod_ref, cab_ref, lnw_ref, lnb_ref, wout_ref, n2w_ref,
              x1_ref, h_ref, *, aw, bw):
    ya = ya_pre + cab_ref[...]
    mu = jnp.mean(ya, axis=-1, keepdims=True)
    var = jnp.mean(jnp.square(ya - mu), axis=-1, keepdims=True)
    ya = (ya - mu) * lax.rsqrt(var + EPS) * lnw_ref[...] + lnb_ref[...]
    ya = ya * jax.nn.sigmoid(ya)
    yc = gb * yc_pre
    y = jnp.dot(ya.astype(BF16), wout_ref[0:aw, :], preferred_element_type=F32)
    y += jnp.dot(yb.astype(BF16), wout_ref[aw:aw + bw, :], preferred_element_type=F32)
    y += jnp.dot(yc.astype(BF16), wout_ref[aw + bw:, :], preferred_element_type=F32)
    x1 = x + mod_ref[2] * y
    x1_ref[...] = x1
    ms = jnp.mean(x1 * x1, axis=-1, keepdims=True)
    h = x1 * lax.rsqrt(ms + EPS) * n2w_ref[...]
    h_ref[...] = (h * (1.0 + mod_ref[4]) + mod_ref[3]).astype(h_ref.dtype)


def _mix_flat_kernel(x_ref, mod_ref, agb_ref, halo_ref, yb_ref, caw_ref, cab_ref, lnw_ref, lnb_ref,
                     ccw_ref, wout_ref, n2w_ref, x1_ref, h_ref, buf, *, aw, cw, bw, tiles_per_seq):
    tm = x_ref.shape[0]
    first = pl.program_id(0) % tiles_per_seq == 0
    halo = halo_ref[:, 0:aw + cw]
    buf[0:HALO, :] = jnp.where(first, jnp.zeros_like(halo), halo)
    buf[HALO:, :] = agb_ref[:, 0:aw + cw]
    ya = jnp.zeros((tm, aw), F32)
    for j in range(KA):
        off = HALO - (KA - 1) + j
        ya += caw_ref[j:j + 1, :] * buf[off:off + tm, 0:aw]
    yc = jnp.zeros((tm, cw), F32)
    for j in range(KC):
        off = HALO - (KC - 1) + j
        yc += ccw_ref[j:j + 1, :] * buf[off:off + tm, aw:aw + cw]
    gb = agb_ref[:, aw + cw:aw + 2 * cw]
    _mix_tail(x_ref[...], ya, yc, gb, yb_ref[...], mod_ref, cab_ref, lnw_ref, lnb_ref, wout_ref,
              n2w_ref, x1_ref, h_ref, aw=aw, bw=bw)


def _mix_hist_kernel(x_ref, mod_ref, af_ref, gf_ref, gb_ref, yb_ref, caw_ref, cab_ref, lnw_ref,
                     lnb_ref, ccw_ref, wout_ref, n2w_ref, x1_ref, h_ref, *, aw, cw, bw):
    ts = af_ref.shape[0] - (KA - 1)
    nb = af_ref.shape[1]
    ya = jnp.zeros((ts, nb, aw), F32)
    for j in range(KA):
        ya += caw_ref[j:j + 1, :][None] * af_ref[j:j + ts]
    yc = jnp.zeros((ts, nb, cw), F32)
    for j in range(KC):
        yc += ccw_ref[j:j + 1, :][None] * gf_ref[j:j + ts]
    _mix_tail(x_ref[...], ya.reshape(ts * nb, aw), yc.reshape(ts * nb, cw), gb_ref[...],
              yb_ref[...], mod_ref, cab_ref, lnw_ref, lnb_ref, wout_ref, n2w_ref, x1_ref, h_ref,
              aw=aw, bw=bw)


def _mix_weight_specs(lw, d, aw, cw):
    return [_const_spec((KA, aw)), _const_spec((1, aw)), _const_spec((1, aw)), _const_spec((1, aw)),
            _const_spec((KC, cw)), _const_spec((d, d)), _const_spec((1, d))]


def _mix_weights(lw):
    return (lw["conv_a_w"], lw["conv_a_b"], lw["ln_a_w"], lw["ln_a_b"], lw["conv_c_w"],
            lw["w_out_b"], lw["norm2_w"])


def _mix_flat(x, mod, agb, yb, lw, *, tm, tiles_per_seq):
    n, d = x.shape
    aw, bw = d // 4, d // 2
    cw = d - aw - bw
    row = lambda w: pl.BlockSpec((tm, w), lambda i: (i, 0))
    hpt = tm // HALO
    halo = pl.BlockSpec((HALO, aw + 2 * cw), lambda i: (jnp.maximum(i * hpt - 1, 0), 0))
    kern = functools.partial(_mix_flat_kernel, aw=aw, cw=cw, bw=bw, tiles_per_seq=tiles_per_seq)
    return pl.pallas_call(
        kern,
        grid=(n // tm,),
        in_specs=[row(d),
                  pl.BlockSpec((None, 6, 1, d), lambda i: (i // tiles_per_seq, 0, 0, 0)),
                  row(aw + 2 * cw), halo, row(bw)] + _mix_weight_specs(lw, d, aw, cw),
        out_specs=[row(d), row(d)],
        out_shape=[jax.ShapeDtypeStruct((n, d), F32), jax.ShapeDtypeStruct((n, d), BF16)],
        scratch_shapes=[pltpu.VMEM((HALO + tm, aw + cw), F32)],
        compiler_params=_params("arbitrary"),
        name="mix_flat",
    )(x, mod, agb, agb, yb, *_mix_weights(lw))


def _mix_hist(x, mod, af, gf, gb, yb, lw):
    n, d = x.shape
    aw, bw = d // 4, d // 2
    cw = d - aw - bw
    full = lambda a: pl.BlockSpec(a.shape, lambda i: (0,) * a.ndim)
    kern = functools.partial(_mix_hist_kernel, aw=aw, cw=cw, bw=bw)
    return pl.pallas_call(
        kern,
        grid=(1,),
        in_specs=[full(x), pl.BlockSpec((None,) + mod.shape[1:], lambda i: (0, 0, 0, 0)),
                  full(af), full(gf), full(gb), full(yb)] + _mix_weight_specs(lw, d, aw, cw),
        out_specs=[pl.BlockSpec((n, d), lambda i: (0, 0))] * 2,
        out_shape=[jax.ShapeDtypeStruct((n, d), F32), jax.ShapeDtypeStruct((n, d), BF16)],
        compiler_params=_params("arbitrary"),
        name="mix_hist",
    )(x, mod, af, gf, gb, yb, *_mix_weights(lw))


def _mlp_kernel(x_ref, h_ref, mod_ref, wup_ref, wdn_ref, o_ref, *, fc):
    h = h_ref[...]
    f = wup_ref.shape[1]
    y = jnp.zeros(x_ref.shape, F32)
    for c0 in range(0, f, fc):
        hid = jnp.dot(h, wup_ref[:, c0:c0 + fc], preferred_element_type=F32)
        hid = jnp.square(jnp.maximum(hid, 0.0)).astype(BF16)
        y += jnp.dot(hid, wdn_ref[c0:c0 + fc, :], preferred_element_type=F32)
    o_ref[...] = x_ref[...] + mod_ref[5] * y


def _mlp(x1, h, mod, wup_b, wdn_b, *, tm, tiles_per_seq):
    n, d = x1.shape
    f = wup_b.shape[1]
    r = mod.shape[2]
    row = pl.BlockSpec((tm, d), lambda i: (i, 0))
    return pl.pallas_call(
        functools.partial(_mlp_kernel, fc=min(f, 1024)),
        grid=(n // tm,),
        in_specs=[row, row,
                  pl.BlockSpec((None, 6, r, d), lambda i: (i // tiles_per_seq, 0, 0, 0)),
                  _const_spec((d, f)), _const_spec((f, d))],
        out_specs=row,
        out_shape=jax.ShapeDtypeStruct((n, d), F32),
        compiler_params=_params("arbitrary"),
        name="mlp",
    )(x1, h, mod, wup_b, wdn_b)


def _group_matrix(width):
    g = jnp.arange(width) // QK_DIM
    return jnp.where(g[:, None] == g[None, :], 1.0 / QK_DIM, 0.0).astype(BF16)


def kernel(x_prompt, x_sample, cache_k, cache_v, state_conv_a, state_conv_c, page_table, c_prompt, c_sample, norm1_w, norm2_w, w_ada, b_ada, w_in, conv_a_w, conv_a_b, ln_a_w, ln_a_b, q_norm_w, k_norm_w, lambda_q1, lambda_k1, lambda_q2, lambda_k2, subln_w, conv_c_w, w_out, w_mlp_up, w_mlp_down):
    depth = w_in.shape[0]
    bp, seq, d = x_prompt.shape
    db, ts, _ = x_sample.shape
    aw, bw = d // 4, d // 2
    cw = d - aw - bw
    heads = bw // V_DIM
    hc = heads * V_DIM
    n_pages, page = page_table.shape[1], cache_k.shape[2]
    past = n_pages * page
    tm = min(seq, 512)
    assert seq % tm == 0 and tm % HALO == 0 and (ts * db) % 8 == 0 and db % 8 == 0
    tiles = seq // tm

    n_c = bp + db
    n_cp = -(-n_c // 8) * 8
    c_all = jnp.zeros((n_cp, d), F32).at[:n_c].set(jnp.concatenate([c_prompt, c_sample], 0))
    mod_all = _ada_mod(c_all, w_ada, b_ada).reshape(depth, n_cp, 6, d)
    mod_p = mod_all[:, :bp, :, None, :]
    mod_s = jnp.tile(mod_all[:, bp:n_c], (1, ts, 1, 1)).transpose(0, 2, 1, 3)[:, None]

    tab_p = _rope_tables(jnp.arange(seq))
    pos_s = jnp.repeat(past + jnp.arange(ts), db)
    tab_s = _rope_tables(pos_s)

    gmat = _group_matrix(2 * LANES)
    kt_pages = cache_k.transpose(0, 1, 3, 4, 5, 2).reshape(depth, cache_k.shape[1], hc, page)
    v_pages = cache_v.reshape(depth, cache_v.shape[1], page * heads, V_DIM)

    lane = jnp.arange(hc)
    sel = (lane[None, None, :] // QK_DIM
           == (jnp.arange(heads)[:, None, None] * 2 + jnp.arange(2)[None, :, None])).astype(BF16)

    xp = x_prompt.reshape(bp * seq, d)
    xs = x_sample.transpose(1, 0, 2).reshape(ts * db, d)
    outs = {k: [] for k in ("kp", "vp", "ap", "cp", "ks", "vs", "as", "cs")}
    for l in range(depth):
        lam_init = 0.8 - 0.6 * math.exp(-0.3 * l)
        lw = {"conv_a_w": conv_a_w[l], "conv_a_b": conv_a_b[l][None], "ln_a_w": ln_a_w[l][None],
              "ln_a_b": ln_a_b[l][None], "conv_c_w": conv_c_w[l], "w_out_b": w_out[l].astype(BF16),
              "norm2_w": norm2_w[l][None]}
        win_b = w_in[l].astype(BF16)
        wup_b, wdn_b = w_mlp_up[l].astype(BF16), w_mlp_down[l].astype(BF16)
        n1w = norm1_w[l][None]
        qkw = jnp.concatenate([jnp.tile(q_norm_w[l], 2 * heads), jnp.tile(k_norm_w[l], 2 * heads)])[None]
        lp = jnp.zeros((8, LANES), F32).at[:4, :QK_DIM].set(
            jnp.stack([lambda_q1[l], lambda_k1[l], lambda_q2[l], lambda_k2[l]]))
        sw = subln_w[l][None]

        agb, qt, kt, kb, v, vt = _inproj(xp, mod_p[l], n1w, win_b, gmat, qkw, tab_p,
                                         tm=tm, tiles_per_seq=tiles, table_tiles=tiles)
        yb = _flash(qt, kb, vt, lp, sw, batch=bp, seq=seq, lam_init=lam_init)
        x1, h = _mix_flat(xp, mod_p[l], agb, yb, lw, tm=tm, tiles_per_seq=tiles)
        xp = _mlp(x1, h, mod_p[l], wup_b, wdn_b, tm=tm, tiles_per_seq=tiles)
        outs["kp"].append(kt.reshape(bp, heads, 2, QK_DIM, seq).transpose(0, 4, 1, 2, 3))
        outs["vp"].append(v.reshape(bp, seq, heads, V_DIM))
        agb3 = agb.reshape(bp, seq, aw + 2 * cw)
        if seq >= KA - 1:
            outs["ap"].append(agb3[:, seq - (KA - 1):, :aw])
            outs["cp"].append(agb3[:, seq - (KC - 1):, aw:aw + cw])
        else:
            za = jnp.zeros((bp, KA - 1 - seq, aw), F32)
            zc = jnp.zeros((bp, max(KC - 1 - seq, 0), cw), F32)
            outs["ap"].append(jnp.concatenate([za, agb3[..., :aw]], 1))
            outs["cp"].append(jnp.concatenate([zc, agb3[..., aw:aw + cw]], 1)[:, -(KC - 1):])

        n_s = ts * db
        agb, qt, kt, kb, v, vt = _inproj(xs, mod_s[l], n1w, win_b, gmat, qkw, tab_s,
                                         tm=n_s, tiles_per_seq=1, table_tiles=1)
        k_seq = kt[0].T.reshape(ts, db, hc).transpose(1, 0, 2)
        v_seq = v.reshape(ts, db, hc).transpose(1, 0, 2)
        q_seq = qt[0].T.reshape(ts, db, hc).transpose(1, 0, 2)
        q_pad = jnp.zeros((db, 8, hc), BF16).at[:, :ts].set(q_seq)
        qblk = (q_pad[:, None, None] * sel[None, :, :, None, :]).reshape(db, 2 * heads * 8, hc)
        knew = jnp.zeros((db, LANES, hc), F32).at[:, :ts].set(k_seq)
        vnew = jnp.zeros((db, LANES, hc), F32).at[:, :ts].set(v_seq)
        o = _paged(qblk, knew, vnew, kt_pages, v_pages, page_table, lp, sw,
                   layer=l, n_new=ts, lam_init=lam_init)
        yb = o[:, :ts].transpose(1, 0, 2).reshape(n_s, hc)
        af = jnp.concatenate([state_conv_a[l].transpose(1, 0, 2), agb[:, :aw].reshape(ts, db, aw)], 0)
        gf = jnp.concatenate([state_conv_c[l].transpose(1, 0, 2),
                              agb[:, aw:aw + cw].reshape(ts, db, cw)], 0)
        x1, h = _mix_hist(xs, mod_s[l], af, gf, agb[:, aw + cw:], yb, lw)
        xs = _mlp(x1, h, mod_s[l], wup_b, wdn_b, tm=n_s, tiles_per_seq=1)
        outs["ks"].append(k_seq.reshape(db, ts, heads, 2, QK_DIM))
        outs["vs"].append(v_seq.reshape(db, ts, heads, V_DIM))
        outs["as"].append(af[-(KA - 1):].transpose(1, 0, 2))
        outs["cs"].append(gf[-(KC - 1):].transpose(1, 0, 2))

    st = lambda key: jnp.stack(outs[key])
    return (xp.reshape(bp, seq, d), xs.reshape(ts, db, d).transpose(1, 0, 2),
            st("kp"), st("vp"), st("ap"), st("cp"), st("ks"), st("vs"), st("as"), st("cs"))
```

```python
import functools
import math

import jax
import jax.numpy as jnp
from jax import lax
from jax.experimental import pallas as pl
from jax.experimental.pallas import tpu as pltpu

F32 = jnp.float32
BF16 = jnp.bfloat16

QK_DIM = 64
V_DIM = 2 * QK_DIM
ROT_DIM = QK_DIM // 4
ROPE_THETA = 500000.0
KA = 31
KC = 3
EPS = 1e-6
NEG_INF = -1e30
Q_SCALE = QK_DIM ** -0.5 * math.log2(math.e)
LANES = 128
HALO = 32
VMEM_LIMIT = 56 * 1024 * 1024


def _params(*sem):
    return pltpu.CompilerParams(dimension_semantics=sem, vmem_limit_bytes=VMEM_LIMIT)


def _const_spec(shape):
    nd = len(shape)
    return pl.BlockSpec(shape, lambda *_: (0,) * nd, pipeline_mode=pl.Buffered(1))


def _ada_kernel(c_ref, w_ref, b_ref, o_ref):
    c = c_ref[...]
    s = (c * jax.nn.sigmoid(c)).astype(BF16)
    o_ref[...] = jnp.dot(s, w_ref[...].astype(BF16), preferred_element_type=F32) + b_ref[...]


def _ada_mod(c_all, w_ada, b_ada):
    depth, d, n = w_ada.shape
    m = c_all.shape[0]
    tn = 1536
    assert n % tn == 0 and m % 8 == 0
    return pl.pallas_call(
        _ada_kernel,
        grid=(depth, n // tn),
        in_specs=[pl.BlockSpec((m, d), lambda l, j: (0, 0)),
                  pl.BlockSpec((None, d, tn), lambda l, j: (l, 0, j)),
                  pl.BlockSpec((None, 1, tn), lambda l, j: (l, 0, j))],
        out_specs=pl.BlockSpec((None, m, tn), lambda l, j: (l, 0, j)),
        out_shape=jax.ShapeDtypeStruct((depth, m, n), F32),
        compiler_params=_params("arbitrary", "arbitrary"),
        name="ada_mod",
    )(c_all, w_ada, b_ada.reshape(depth, 1, n))


def _rope_kernel(ang_ref, cos_ref, sa_ref, sb_ref):
    ang = ang_ref[...]
    d = lax.broadcasted_iota(jnp.int32, ang.shape, 1) % QK_DIM
    half = ROT_DIM // 2
    c, s = jnp.cos(ang), jnp.sin(ang)
    cos_ref[...] = jnp.where(d < ROT_DIM, c, 1.0)
    sa_ref[...] = jnp.where((d >= half) & (d < ROT_DIM), s, 0.0)
    sb_ref[...] = jnp.where(d < half, -s, 0.0)


def _rope_tables(pos):
    n = pos.shape[0]
    half = ROT_DIM // 2
    inv_freq = ROPE_THETA ** (-jnp.arange(half, dtype=F32) / half)
    lane_freq = inv_freq[jnp.arange(LANES) % half]
    ang = pos.astype(F32)[:, None] * lane_freq[None, :]
    tr = min(n, 1024)
    assert n % tr == 0 and tr % 8 == 0
    spec = pl.BlockSpec((tr, LANES), lambda i: (i, 0))
    return pl.pallas_call(
        _rope_kernel, grid=(n // tr,), in_specs=[spec], out_specs=[spec] * 3,
        out_shape=[jax.ShapeDtypeStruct((n, LANES), F32)] * 3,
        compiler_params=_params("arbitrary"), name="rope_tables",
    )(ang)


def _inproj_kernel(x_ref, mod_ref, n1w_ref, win_ref, gmat_ref, qkw_ref, cos_ref, sa_ref, sb_ref,
                   agb_ref, qt_ref, kt_ref, kb_ref, v_ref, vt_ref, z_even, z_odd,
                   *, aw, qc, vc, cw):
    tm = x_ref.shape[0]
    s = pl.program_id(0)

    @pl.when(s == 0)
    def _():
        z_odd[...] = jnp.zeros_like(z_odd)

    def step(z_w, z_r):
        x = x_ref[...]
        ms = jnp.mean(x * x, axis=-1, keepdims=True)
        u = x * lax.rsqrt(ms + EPS) * n1w_ref[...]
        u = u * (1.0 + mod_ref[1]) + mod_ref[0]
        z = jnp.dot(u.astype(BF16), win_ref[...], preferred_element_type=F32)
        ncols = z.shape[1]
        z_w[:, 0:ncols] = z
        o_q, o_v = 2 * aw, 2 * aw + 2 * qc
        o_gb = o_v + vc
        gm = gmat_ref[...]
        gw = gm.shape[0]
        for c0 in range(0, 2 * qc, gw):
            y = z[:, o_q + c0:o_q + c0 + gw]
            z_w[:, ncols + c0:ncols + c0 + gw] = jnp.dot((y * y).astype(BF16), gm,
                                                         preferred_element_type=F32)

        zcols = lambda a, b: z_r[:, a:b]

        za = zcols(0, 2 * aw)
        agb_ref[:, 0:aw] = za[:, 0:aw] * jax.nn.sigmoid(za[:, aw:2 * aw])

        v = zcols(o_v, o_v + vc)
        n_heads = vc // LANES
        for h in range(n_heads):
            vh = v[:, h * LANES:(h + 1) * LANES]
            v_ref[pl.ds(h, tm, stride=n_heads), :] = vh
            vt_ref[h * LANES:(h + 1) * LANES, :] = vh.T.astype(BF16)

        zg = zcols(o_gb, o_gb + 3 * cw)
        agb_ref[:, aw:aw + cw] = zg[:, cw:2 * cw] * zg[:, 2 * cw:3 * cw]
        agb_ref[:, aw + cw:aw + 2 * cw] = zg[:, 0:cw]

        cos, sa, sb = cos_ref[...], sa_ref[...], sb_ref[...]
        half = ROT_DIM // 2
        for c0 in range(0, 2 * qc, gw):
            y = zcols(o_q + c0, o_q + c0 + gw)
            gms = zcols(ncols + c0, ncols + c0 + gw)
            y = y * lax.rsqrt(gms + EPS) * qkw_ref[:, c0:c0 + gw]
            for c1 in range(0, gw, LANES):
                yc = y[:, c1:c1 + LANES]
                r = (yc * cos + pltpu.roll(yc, half, 1) * sa + pltpu.roll(yc, LANES - half, 1) * sb)
                col = c0 + c1
                if col < qc:
                    qt_ref[col:col + LANES, :] = (r * Q_SCALE).T.astype(BF16)
                else:
                    kt_ref[col - qc:col - qc + LANES, :] = r.T
                    kb_ref[:, col - qc:col - qc + LANES] = r.astype(BF16)

    @pl.when(s % 2 == 0)
    def _():
        step(z_even, z_odd)

    @pl.when(s % 2 == 1)
    def _():
        step(z_odd, z_even)


def _inproj(x, mod, n1w, win_b, gmat, qkw, tables, *, tm, tiles_per_seq, table_tiles):
    n, d = x.shape
    aw, qc = d // 4, d // 2
    vc, cw = d // 2, d - d // 4 - d // 2
    ncols = win_b.shape[1]
    r = mod.shape[2]
    nt = n // tm
    cur = lambda s: jnp.minimum(s, nt - 1)
    prv = lambda s: jnp.maximum(s - 1, 0)
    row = lambda w: pl.BlockSpec((tm, w), lambda s: (prv(s), 0))
    tab = pl.BlockSpec((tm, LANES), lambda s: (prv(s) % table_tiles, 0))
    tile_t = lambda w: pl.BlockSpec((None, w, tm), lambda s: (prv(s), 0, 0))
    kern = functools.partial(_inproj_kernel, aw=aw, qc=qc, vc=vc, cw=cw)
    return pl.pallas_call(
        kern,
        grid=(nt + 1,),
        in_specs=[pl.BlockSpec((tm, d), lambda s: (cur(s), 0)),
                  pl.BlockSpec((None, 6, r, d), lambda s: (cur(s) // tiles_per_seq, 0, 0, 0)),
                  _const_spec((1, d)), _const_spec((d, ncols)), _const_spec(gmat.shape),
                  _const_spec((1, 2 * qc)), tab, tab, tab],
        out_specs=[row(aw + 2 * cw), tile_t(qc),
                   pl.BlockSpec((None, qc, tm),
                                lambda s: (prv(s) // tiles_per_seq, 0, prv(s) % tiles_per_seq)),
                   row(qc), pl.BlockSpec((tm * (vc // LANES), LANES), lambda s: (prv(s), 0)),
                   tile_t(vc)],
        scratch_shapes=[pltpu.VMEM((tm, ncols + 2 * qc), F32)] * 2,
        out_shape=[jax.ShapeDtypeStruct((n, aw + 2 * cw), F32),
                   jax.ShapeDtypeStruct((n // tm, qc, tm), BF16),
                   jax.ShapeDtypeStruct((n // (tm * tiles_per_seq), qc, tm * tiles_per_seq), F32),
                   jax.ShapeDtypeStruct((n, qc), BF16),
                   jax.ShapeDtypeStruct((n * (vc // LANES), LANES), F32),
                   jax.ShapeDtypeStruct((n // tm, vc, tm), BF16)],
        compiler_params=_params("arbitrary"),
        name="inproj",
    )(x, mod, n1w, win_b, gmat, qkw, *tables)


def _lambda(lp_ref, lam_init):
    s1 = jnp.sum(lp_ref[0:1, :] * lp_ref[1:2, :], axis=-1, keepdims=True)
    s2 = jnp.sum(lp_ref[2:3, :] * lp_ref[3:4, :], axis=-1, keepdims=True)
    return jnp.exp(s1) - jnp.exp(s2) + lam_init


def _subln(o, w, lam_init):
    ms = jnp.mean(o * o, axis=-1, keepdims=True)
    return o * lax.rsqrt(ms + EPS) * w * (1.0 - lam_init)


_NT = (((1,), (1,)), ((), ()))


def _flash_kernel(qt_ref, k_ref, vt_ref, lp_ref, sw_ref, o_ref, qc_ref, sa_ref, sb_ref, xa_ref, xb_ref,
                  m_ref, l_ref, acc_ref, *, lam_init):
    i = pl.program_id(2)
    tk = k_ref.shape[0] // vt_ref.shape[0]
    qt = qt_ref[...]
    row = lax.broadcasted_iota(jnp.int32, qt.shape, 0)
    zero = jnp.zeros_like(qt)
    qc_ref[0] = jnp.where(row < QK_DIM, qt, zero)
    qc_ref[1] = jnp.where(row >= QK_DIM, qt, zero)
    m_ref[...] = jnp.full_like(m_ref, NEG_INF)
    l_ref[...] = jnp.zeros_like(l_ref)
    acc_ref[...] = jnp.zeros_like(acc_ref)

    def produce(j, s_ref, x_ref):
        k = k_ref[pl.ds(pl.multiple_of(j * tk, tk), tk), :]
        for c in range(2):
            s = jnp.dot(k, qc_ref[c], preferred_element_type=F32)
            s_ref[c] = s
            x_ref[c] = jnp.max(s, axis=0, keepdims=True)

    def consume(j, s_ref, x_ref, diag):
        vt = vt_ref[j]
        for c in range(2):
            s = s_ref[c]
            if diag:
                kr = lax.broadcasted_iota(jnp.int32, s.shape, 0)
                qq = lax.broadcasted_iota(jnp.int32, s.shape, 1)
                s = jnp.where(kr <= qq, s, NEG_INF)
                x = jnp.max(s, axis=0, keepdims=True)
            else:
                x = x_ref[c]
            m_prev = m_ref[c]
            m_new = jnp.maximum(m_prev, x)
            alpha = jnp.exp2(m_prev - m_new)
            p = jnp.exp2(s - m_new)
            l_ref[c] = alpha * l_ref[c] + jnp.sum(p, axis=0, keepdims=True)
            acc_ref[c] = alpha * acc_ref[c] + jnp.dot(vt, p.astype(BF16), preferred_element_type=F32)
            m_ref[c] = m_new

    produce(0, sa_ref, xa_ref)

    def pair(t, carry):
        j = 2 * t
        produce(j + 1, sb_ref, xb_ref)
        consume(j, sa_ref, xa_ref, False)
        produce(j + 2, sa_ref, xa_ref)
        consume(j + 1, sb_ref, xb_ref, False)
        return carry

    lax.fori_loop(0, i // 2, pair, 0)

    @pl.when(i % 2 == 0)
    def _():
        consume(i, sa_ref, xa_ref, True)

    @pl.when(i % 2 == 1)
    def _():
        produce(i, sb_ref, xb_ref)
        consume(i - 1, sa_ref, xa_ref, False)
        consume(i, sb_ref, xb_ref, True)

    lam = _lambda(lp_ref, lam_init)
    ot = acc_ref[0] / l_ref[0] - lam * (acc_ref[1] / l_ref[1])
    o_ref[...] = _subln(ot.T, sw_ref[...], lam_init).astype(o_ref.dtype)


def _flash(qt, kb, vt, lp, sw, *, batch, seq, lam_init):
    n, hc = kb.shape
    heads = hc // LANES
    tq, tk = qt.shape[2], vt.shape[2]
    assert tq == tk
    nq, nk = seq // tq, seq // tk
    cspec = lambda shape: pl.BlockSpec(shape, lambda b, h, i: (0, 0))
    stat = pltpu.VMEM((2, 1, tq), F32)
    score = pltpu.VMEM((2, tk, tq), F32)
    return pl.pallas_call(
        functools.partial(_flash_kernel, lam_init=lam_init),
        grid=(batch, heads, nq),
        in_specs=[pl.BlockSpec((None, LANES, tq), lambda b, h, i: (b * nq + i, h, 0)),
                  pl.BlockSpec((seq, LANES), lambda b, h, i: (b, h)),
                  pl.BlockSpec((nk, LANES, tk), lambda b, h, i: (b, h, 0)),
                  cspec((8, LANES)), cspec((1, LANES))],
        out_specs=pl.BlockSpec((tq, LANES), lambda b, h, i: (b * nq + i, h)),
        scratch_shapes=[pltpu.VMEM((2, LANES, tq), BF16), score, score, stat, stat, stat, stat,
                        pltpu.VMEM((2, V_DIM, tq), F32)],
        out_shape=jax.ShapeDtypeStruct((n, hc), BF16),
        compiler_params=_params("arbitrary", "arbitrary", "arbitrary"),
        name="flash_diff_attn",
    )(qt, kb, vt, lp, sw)


def _paged_kernel(pt_ref, q_ref, kn_ref, vn_ref, lp_ref, sw_ref, *rest,
                  pages, n_new, lam_init, heads):
    k_refs, v_refs = rest[:pages], rest[pages:2 * pages]
    o_ref, m, l, acc = rest[2 * pages:]
    c = pl.program_id(1)
    page = k_refs[0].shape[1]
    rows_h = 2 * 8

    @pl.when(c == 0)
    def _():
        m[...] = jnp.full_like(m, NEG_INF)
        l[...] = jnp.zeros_like(l)
        acc[...] = jnp.zeros_like(acc)

    def update(s, v_of_head):
        m_prev = m[...]
        m_new = jnp.maximum(m_prev, jnp.max(s, axis=-1, keepdims=True))
        alpha = jnp.exp2(m_prev - m_new)
        p = jnp.exp2(s - m_new)
        l[...] = alpha * l[...] + jnp.sum(p, axis=-1, keepdims=True)
        m[...] = m_new
        pb = p.astype(BF16)
        for h in range(heads):
            r = slice(h * rows_h, (h + 1) * rows_h)
            acc[r, :] = alpha[r, :] * acc[r, :] + jnp.dot(pb[r, :], v_of_head(h),
                                                          preferred_element_type=F32)

    q = q_ref[...]
    kcat = jnp.concatenate([r[...].astype(BF16) for r in k_refs], axis=1)
    s = jnp.dot(q, kcat, preferred_element_type=F32)
    update(s, lambda h: jnp.concatenate(
        [r[pl.ds(h, page, stride=heads), :].astype(BF16) for r in v_refs], axis=0))

    @pl.when(c == pl.num_programs(1) - 1)
    def _():
        s = lax.dot_general(q, kn_ref[...].astype(BF16), _NT, preferred_element_type=F32)
        t_q = lax.broadcasted_iota(jnp.int32, s.shape, 0) % 8
        t_k = lax.broadcasted_iota(jnp.int32, s.shape, 1)
        s = jnp.where((t_k <= t_q) & (t_k < n_new), s, NEG_INF)
        update(s, lambda h: vn_ref[:, h * LANES:(h + 1) * LANES].astype(BF16))
        lam = _lambda(lp_ref, lam_init)
        for h in range(heads):
            r1 = slice(h * rows_h, h * rows_h + 8)
            r2 = slice(h * rows_h + 8, (h + 1) * rows_h)
            o = acc[r1, :] / l[r1, :] - lam * (acc[r2, :] / l[r2, :])
            o_ref[:, h * LANES:(h + 1) * LANES] = _subln(o, sw_ref[...], lam_init)


def _paged(qblk, knew, vnew, kt_pages, v_pages, page_table, lp, sw, *, layer, n_new, lam_init):
    db, rows, hc = qblk.shape
    heads = hc // LANES
    n_pages = page_table.shape[1]
    page = kt_pages.shape[3]
    pages = math.gcd(n_pages, 8)
    chunks = n_pages // pages
    pt = page_table.reshape(-1).astype(jnp.int32)

    def page_spec(i, shape):
        return pl.BlockSpec((None, None) + shape,
                            lambda b, c, pt: (layer, pt[b * n_pages + c * pages + i], 0, 0))

    per_seq = lambda r: pl.BlockSpec((None, r, hc), lambda b, c, pt: (b, 0, 0))
    cspec = lambda shape: pl.BlockSpec(shape, lambda b, c, pt: (0, 0))
    kern = functools.partial(_paged_kernel, pages=pages, n_new=n_new, lam_init=lam_init, heads=heads)
    return pl.pallas_call(
        kern,
        grid_spec=pltpu.PrefetchScalarGridSpec(
            num_scalar_prefetch=1,
            grid=(db, chunks),
            in_specs=[per_seq(rows), per_seq(knew.shape[1]), per_seq(vnew.shape[1]),
                      cspec((8, LANES)), cspec((1, LANES))]
                     + [page_spec(i, (hc, page)) for i in range(pages)]
                     + [page_spec(i, (page * heads, LANES)) for i in range(pages)],
            out_specs=per_seq(8),
            scratch_shapes=[pltpu.VMEM((rows, 1), F32), pltpu.VMEM((rows, 1), F32),
                            pltpu.VMEM((rows, LANES), F32)]),
        out_shape=jax.ShapeDtypeStruct((db, 8, hc), F32),
        compiler_params=_params("arbitrary", "arbitrary"),
        name="paged_diff_attn",
    )(pt, qblk, knew, vnew, lp, sw, *([kt_pages] * pages), *([v_pages] * pages))


def _mix_tail(x, ya_pre, yc_pre, gb, yb, mod_ref, cab_ref, lnw_ref, lnb_ref, wout_ref, n2w_ref,
              x1_ref, h_ref, *, aw, bw):
    ya = ya_pre + cab_ref[...]
    mu = jnp.mean(ya, axis=-1, keepdims=True)
    var = jnp.mean(jnp.square(ya - mu), axis=-1, keepdims=True)
    ya = (ya - mu) * lax.rsqrt(var + EPS) * lnw_ref[...] + lnb_ref[...]
    ya = ya * jax.nn.sigmoid(ya)
    yc = gb * yc_pre
    y = jnp.dot(ya.astype(BF16), wout_ref[0:aw, :], preferred_element_type=F32)
    y += jnp.dot(yb.astype(BF16), wout_ref[aw:aw + bw, :], preferred_element_type=F32)
    y += jnp.dot(yc.astype(BF16), wout_ref[aw + bw:, :], preferred_element_type=F32)
    x1 = x + mod_ref[2] * y
    x1_ref[...] = x1
    ms = jnp.mean(x1 * x1, axis=-1, keepdims=True)
    h = x1 * lax.rsqrt(ms + EPS) * n2w_ref[...]
    h_ref[...] = (h * (1.0 + mod_ref[4]) + mod_ref[3]).astype(h_ref.dtype)


def _mixmlp_kernel(x_ref, mod_ref, modp_ref, agb_ref, halo_ref, yb_ref, caw_ref, cab_ref, lnw_ref,
                   lnb_ref, ccw_ref, wout_ref, n2w_ref, wup_ref, wdn_ref, o_ref, buf, sh,
                   x1_even, h_even, x1_odd, h_odd, *, aw, cw, bw, tiles_per_seq, fc):
    s = pl.program_id(0)
    tm = x_ref.shape[0]

    @pl.when(s == 0)
    def _():
        x1_odd[...] = jnp.zeros_like(x1_odd)
        h_odd[...] = jnp.zeros_like(h_odd)
        buf[HALO + tm:, :] = jnp.zeros((8, aw + cw), F32)

    def step(x1_w, h_w, x1_r, h_r):
        tile = jnp.minimum(s, pl.num_programs(0) - 2)
        first = tile % tiles_per_seq == 0
        halo = halo_ref[:, 0:aw + cw]
        buf[0:HALO, :] = jnp.where(first, jnp.zeros_like(halo), halo)
        buf[HALO:HALO + tm, :] = agb_ref[:, 0:aw + cw]
        for b in range(8):
            sh[b] = buf[b:b + tm + HALO, 0:aw]
        ya = jnp.zeros((tm, aw), F32)
        for j in range(KA):
            a8, b = divmod(HALO - (KA - 1) + j, 8)
            ya += caw_ref[j:j + 1, :] * sh[b, 8 * a8:8 * a8 + tm, :]
        yc = jnp.zeros((tm, cw), F32)
        for j in range(KC):
            off = HALO - (KC - 1) + j
            yc += ccw_ref[j:j + 1, :] * buf[off:off + tm, aw:aw + cw]
        gb = agb_ref[:, aw + cw:aw + 2 * cw]
        _mix_tail(x_ref[...], ya, yc, gb, yb_ref[...], mod_ref, cab_ref, lnw_ref, lnb_ref, wout_ref,
                  n2w_ref, x1_w, h_w, aw=aw, bw=bw)
        o_ref[...] = _mlp_body(x1_r[...], h_r[...], modp_ref[5], wup_ref, wdn_ref, fc)

    @pl.when(s % 2 == 0)
    def _():
        step(x1_even, h_even, x1_odd, h_odd)

    @pl.when(s % 2 == 1)
    def _():
        step(x1_odd, h_odd, x1_even, h_even)


def _mix_hist_kernel(x_ref, mod_ref, af_ref, gf_ref, gb_ref, yb_ref, caw_ref, cab_ref, lnw_ref,
                     lnb_ref, ccw_ref, wout_ref, n2w_ref, x1_ref, h_ref, *, aw, cw, bw):
    ts = af_ref.shape[0] - (KA - 1)
    nb = af_ref.shape[1]
    ya = jnp.zeros((ts, nb, aw), F32)
    for j in range(KA):
        ya += caw_ref[j:j + 1, :][None] * af_ref[j:j + ts]
    yc = jnp.zeros((ts, nb, cw), F32)
    for j in range(KC):
        yc += ccw_ref[j:j + 1, :][None] * gf_ref[j:j + ts]
    _mix_tail(x_ref[...], ya.reshape(ts * nb, aw), yc.reshape(ts * nb, cw), gb_ref[...],
              yb_ref[...], mod_ref, cab_ref, lnw_ref, lnb_ref, wout_ref, n2w_ref, x1_ref, h_ref,
              aw=aw, bw=bw)


def _mix_weight_specs(lw, d, aw, cw):
    return [_const_spec((KA, aw)), _const_spec((1, aw)), _const_spec((1, aw)), _const_spec((1, aw)),
            _const_spec((KC, cw)), _const_spec((d, d)), _const_spec((1, d))]


def _mix_weights(lw):
    return (lw["conv_a_w"], lw["conv_a_b"], lw["ln_a_w"], lw["ln_a_b"], lw["conv_c_w"],
            lw["w_out_b"], lw["norm2_w"])


def _mixmlp(x, mod, agb, yb, lw, wup_b, wdn_b, *, tm, tiles_per_seq):
    n, d = x.shape
    f = wup_b.shape[1]
    aw, bw = d // 4, d // 2
    cw = d - aw - bw
    nt = n // tm
    cur = lambda s: jnp.minimum(s, nt - 1)
    prv = lambda s: jnp.maximum(s - 1, 0)
    row = lambda w: pl.BlockSpec((tm, w), lambda s: (cur(s), 0))
    modspec = lambda t: pl.BlockSpec((None, 6, 1, d), lambda s: (t(s) // tiles_per_seq, 0, 0, 0))
    hpt = tm // HALO
    halo = pl.BlockSpec((HALO, aw + 2 * cw), lambda s: (jnp.maximum(cur(s) * hpt - 1, 0), 0))
    kern = functools.partial(_mixmlp_kernel, aw=aw, cw=cw, bw=bw, tiles_per_seq=tiles_per_seq,
                             fc=min(f, 1024))
    return pl.pallas_call(
        kern,
        grid=(nt + 1,),
        in_specs=[row(d), modspec(cur), modspec(prv), row(aw + 2 * cw), halo, row(bw)]
                 + _mix_weight_specs(lw, d, aw, cw) + [_const_spec((d, f)), _const_spec((f, d))],
        out_specs=pl.BlockSpec((tm, d), lambda s: (prv(s), 0)),
        out_shape=jax.ShapeDtypeStruct((n, d), F32),
        scratch_shapes=[pltpu.VMEM((HALO + tm + 8, aw + cw), F32),
                        pltpu.VMEM((8, HALO + tm, aw), F32),
                        pltpu.VMEM((tm, d), F32), pltpu.VMEM((tm, d), BF16),
                        pltpu.VMEM((tm, d), F32), pltpu.VMEM((tm, d), BF16)],
        compiler_params=_params("arbitrary"),
        name="mix_mlp",
    )(x, mod, mod, agb, agb, yb, *_mix_weights(lw), wup_b, wdn_b)


def _mix_hist(x, mod, af, gf, gb, yb, lw):
    n, d = x.shape
    aw, bw = d // 4, d // 2
    cw = d - aw - bw
    full = lambda a: pl.BlockSpec(a.shape, lambda i: (0,) * a.ndim)
    kern = functools.partial(_mix_hist_kernel, aw=aw, cw=cw, bw=bw)
    return pl.pallas_call(
        kern,
        grid=(1,),
        in_specs=[full(x), pl.BlockSpec((None,) + mod.shape[1:], lambda i: (0, 0, 0, 0)),
                  full(af), full(gf), full(gb), full(yb)] + _mix_weight_specs(lw, d, aw, cw),
        out_specs=[pl.BlockSpec((n, d), lambda i: (0, 0))] * 2,
        out_shape=[jax.ShapeDtypeStruct((n, d), F32), jax.ShapeDtypeStruct((n, d), BF16)],
        compiler_params=_params("arbitrary"),
        name="mix_hist",
    )(x, mod, af, gf, gb, yb, *_mix_weights(lw))


def _mlp_body(x1, h, gate2, wup_ref, wdn_ref, fc):
    f = wup_ref.shape[1]
    y = jnp.zeros(x1.shape, F32)
    for c0 in range(0, f, fc):
        hid = jnp.dot(h, wup_ref[:, c0:c0 + fc], preferred_element_type=F32)
        hid = jnp.square(jnp.maximum(hid, 0.0)).astype(BF16)
        y += jnp.dot(hid, wdn_ref[c0:c0 + fc, :], preferred_element_type=F32)
    return x1 + gate2 * y


def _mlp_kernel(x_ref, h_ref, mod_ref, wup_ref, wdn_ref, o_ref, *, fc):
    o_ref[...] = _mlp_body(x_ref[...], h_ref[...], mod_ref[5], wup_ref, wdn_ref, fc)


def _mlp(x1, h, mod, wup_b, wdn_b, *, tm, tiles_per_seq):
    n, d = x1.shape
    f = wup_b.shape[1]
    r = mod.shape[2]
    row = pl.BlockSpec((tm, d), lambda i: (i, 0))
    return pl.pallas_call(
        functools.partial(_mlp_kernel, fc=min(f, 1024)),
        grid=(n // tm,),
        in_specs=[row, row,
                  pl.BlockSpec((None, 6, r, d), lambda i: (i // tiles_per_seq, 0, 0, 0)),
                  _const_spec((d, f)), _const_spec((f, d))],
        out_specs=row,
        out_shape=jax.ShapeDtypeStruct((n, d), F32),
        compiler_params=_params("arbitrary"),
        name="mlp",
    )(x1, h, mod, wup_b, wdn_b)


def _group_matrix(width):
    g = jnp.arange(width) // QK_DIM
    return jnp.where(g[:, None] == g[None, :], 1.0 / QK_DIM, 0.0).astype(BF16)


def kernel(x_prompt, x_sample, cache_k, cache_v, state_conv_a, state_conv_c, page_table, c_prompt, c_sample, norm1_w, norm2_w, w_ada, b_ada, w_in, conv_a_w, conv_a_b, ln_a_w, ln_a_b, q_norm_w, k_norm_w, lambda_q1, lambda_k1, lambda_q2, lambda_k2, subln_w, conv_c_w, w_out, w_mlp_up, w_mlp_down):
    depth = w_in.shape[0]
    bp, seq, d = x_prompt.shape
    db, ts, _ = x_sample.shape
    aw, bw = d // 4, d // 2
    cw = d - aw - bw
    heads = bw // V_DIM
    hc = heads * V_DIM
    n_pages, page = page_table.shape[1], cache_k.shape[2]
    past = n_pages * page
    tm = min(seq, 512)
    assert seq % tm == 0 and tm % HALO == 0 and (ts * db) % 8 == 0 and db % 8 == 0
    tiles = seq // tm

    n_c = bp + db
    n_cp = -(-n_c // 8) * 8
    c_all = jnp.zeros((n_cp, d), F32).at[:n_c].set(jnp.concatenate([c_prompt, c_sample], 0))
    mod_all = _ada_mod(c_all, w_ada, b_ada).reshape(depth, n_cp, 6, d)
    mod_p = mod_all[:, :bp, :, None, :]
    mod_s = jnp.tile(mod_all[:, bp:n_c], (1, ts, 1, 1)).transpose(0, 2, 1, 3)[:, None]

    tab_p = _rope_tables(jnp.arange(seq))
    pos_s = jnp.repeat(past + jnp.arange(ts), db)
    tab_s = _rope_tables(pos_s)

    gmat = _group_matrix(2 * LANES)
    kt_pages = cache_k.transpose(0, 1, 3, 4, 5, 2).reshape(depth, cache_k.shape[1], hc, page)
    v_pages = cache_v.reshape(depth, cache_v.shape[1], page * heads, V_DIM)

    lane = jnp.arange(hc)
    sel = (lane[None, None, :] // QK_DIM
           == (jnp.arange(heads)[:, None, None] * 2 + jnp.arange(2)[None, :, None])).astype(BF16)

    xp = x_prompt.reshape(bp * seq, d)
    xs = x_sample.transpose(1, 0, 2).reshape(ts * db, d)
    outs = {k: [] for k in ("kp", "vp", "ap", "cp", "ks", "vs", "as", "cs")}
    for l in range(depth):
        lam_init = 0.8 - 0.6 * math.exp(-0.3 * l)
        lw = {"conv_a_w": conv_a_w[l], "conv_a_b": conv_a_b[l][None], "ln_a_w": ln_a_w[l][None],
              "ln_a_b": ln_a_b[l][None], "conv_c_w": conv_c_w[l], "w_out_b": w_out[l].astype(BF16),
              "norm2_w": norm2_w[l][None]}
        win_b = w_in[l].astype(BF16)
        wup_b, wdn_b = w_mlp_up[l].astype(BF16), w_mlp_down[l].astype(BF16)
        n1w = norm1_w[l][None]
        qkw = jnp.concatenate([jnp.tile(q_norm_w[l], 2 * heads), jnp.tile(k_norm_w[l], 2 * heads)])[None]
        lp = jnp.zeros((8, LANES), F32).at[:4, :QK_DIM].set(
            jnp.stack([lambda_q1[l], lambda_k1[l], lambda_q2[l], lambda_k2[l]]))
        sw = subln_w[l][None]

        agb, qt, kt, kb, v, vt = _inproj(xp, mod_p[l], n1w, win_b, gmat, qkw, tab_p,
                                         tm=tm, tiles_per_seq=tiles, table_tiles=tiles)
        yb = _flash(qt, kb, vt, lp, sw, batch=bp, seq=seq, lam_init=lam_init)
        xp = _mixmlp(xp, mod_p[l], agb, yb, lw, wup_b, wdn_b, tm=tm, tiles_per_seq=tiles)
        outs["kp"].append(kt.reshape(bp, heads, 2, QK_DIM, seq).transpose(0, 4, 1, 2, 3))
        outs["vp"].append(v.reshape(bp, seq, heads, V_DIM))
        agb3 = agb.reshape(bp, seq, aw + 2 * cw)
        if seq >= KA - 1:
            outs["ap"].append(agb3[:, seq - (KA - 1):, :aw])
            outs["cp"].append(agb3[:, seq - (KC - 1):, aw:aw + cw])
        else:
            za = jnp.zeros((bp, KA - 1 - seq, aw), F32)
            zc = jnp.zeros((bp, max(KC - 1 - seq, 0), cw), F32)
            outs["ap"].append(jnp.concatenate([za, agb3[..., :aw]], 1))
            outs["cp"].append(jnp.concatenate([zc, agb3[..., aw:aw + cw]], 1)[:, -(KC - 1):])

        n_s = ts * db
        agb, qt, kt, kb, v, vt = _inproj(xs, mod_s[l], n1w, win_b, gmat, qkw, tab_s,
                                         tm=n_s, tiles_per_seq=1, table_tiles=1)
        k_seq = kt[0].T.reshape(ts, db, hc).transpose(1, 0, 2)
        v_seq = v.reshape(ts, db, hc).transpose(1, 0, 2)
        q_seq = qt[0].T.reshape(ts, db, hc).transpose(1, 0, 2)
        q_pad = jnp.zeros((db, 8, hc), BF16).at[:, :ts].set(q_seq)
        qblk = (q_pad[:, None, None] * sel[None, :, :, None, :]).reshape(db, 2 * heads * 8, hc)
        knew = jnp.zeros((db, LANES, hc), F32).at[:, :ts].set(k_seq)
        vnew = jnp.zeros((db, LANES, hc), F32).at[:, :ts].set(v_seq)
        o = _paged(qblk, knew, vnew, kt_pages, v_pages, page_table, lp, sw,
                   layer=l, n_new=ts, lam_init=lam_init)
        yb = o[:, :ts].transpose(1, 0, 2).reshape(n_s, hc)
        af = jnp.concatenate([state_conv_a[l].transpose(1, 0, 2), agb[:, :aw].reshape(ts, db, aw)], 0)
        gf = jnp.concatenate([state_conv_c[l].transpose(1, 0, 2),
                              agb[:, aw:aw + cw].reshape(ts, db, cw)], 0)
        x1, h = _mix_hist(xs, mod_s[l], af, gf, agb[:, aw + cw:], yb, lw)
        xs = _mlp(x1, h, mod_s[l], wup_b, wdn_b, tm=n_s, tiles_per_seq=1)
        outs["ks"].append(k_seq.reshape(db, ts, heads, 2, QK_DIM))
        outs["vs"].append(v_seq.reshape(db, ts, heads, V_DIM))
        outs["as"].append(af[-(KA - 1):].transpose(1, 0, 2))
        outs["cs"].append(gf[-(KC - 1):].transpose(1, 0, 2))

    st = lambda key: jnp.stack(outs[key])
    return (xp.reshape(bp, seq, d), xs.reshape(ts, db, d).transpose(1, 0, 2),
            st("kp"), st("vp"), st("ap"), st("cp"), st("ks"), st("vs"), st("as"), st("cs"))
```

```python
import functools
import math

import jax
import jax.numpy as jnp
from jax import lax
from jax.experimental import pallas as pl
from jax.experimental.pallas import tpu as pltpu

F32 = jnp.float32
BF16 = jnp.bfloat16

QK_DIM = 64
V_DIM = 2 * QK_DIM
ROT_DIM = QK_DIM // 4
ROPE_THETA = 500000.0
KA = 31
KC = 3
EPS = 1e-6
NEG_INF = -1e30
Q_SCALE = QK_DIM ** -0.5 * math.log2(math.e)
LANES = 128
HALO = 32
PAGE_BUFFERS = 3
FLASH_HEADS = 2
VMEM_LIMIT = 56 * 1024 * 1024


def _params(*sem):
    return pltpu.CompilerParams(dimension_semantics=sem, vmem_limit_bytes=VMEM_LIMIT)


def _const_spec(shape):
    nd = len(shape)
    return pl.BlockSpec(shape, lambda *_: (0,) * nd, pipeline_mode=pl.Buffered(1))


def _ada_kernel(c_ref, w_ref, b_ref, o_ref):
    c = c_ref[...]
    s = (c * jax.nn.sigmoid(c)).astype(BF16)
    o_ref[...] = jnp.dot(s, w_ref[...].astype(BF16), preferred_element_type=F32) + b_ref[...]


def _ada_mod(c_all, w_ada, b_ada):
    depth, d, n = w_ada.shape
    m = c_all.shape[0]
    tn = 1536
    assert n % tn == 0 and m % 8 == 0
    return pl.pallas_call(
        _ada_kernel,
        grid=(depth, n // tn),
        in_specs=[pl.BlockSpec((m, d), lambda l, j: (0, 0)),
                  pl.BlockSpec((None, d, tn), lambda l, j: (l, 0, j)),
                  pl.BlockSpec((None, 1, tn), lambda l, j: (l, 0, j))],
        out_specs=pl.BlockSpec((None, m, tn), lambda l, j: (l, 0, j)),
        out_shape=jax.ShapeDtypeStruct((depth, m, n), F32),
        compiler_params=_params("arbitrary", "arbitrary"),
        name="ada_mod",
    )(c_all, w_ada, b_ada.reshape(depth, 1, n))


def _rope_kernel(ang_ref, cos_ref, sa_ref, sb_ref):
    ang = ang_ref[...]
    d = lax.broadcasted_iota(jnp.int32, ang.shape, 1) % QK_DIM
    half = ROT_DIM // 2
    c, s = jnp.cos(ang), jnp.sin(ang)
    cos_ref[...] = jnp.where(d < ROT_DIM, c, 1.0)
    sa_ref[...] = jnp.where((d >= half) & (d < ROT_DIM), s, 0.0)
    sb_ref[...] = jnp.where(d < half, -s, 0.0)


def _rope_tables(pos):
    n = pos.shape[0]
    half = ROT_DIM // 2
    inv_freq = ROPE_THETA ** (-jnp.arange(half, dtype=F32) / half)
    lane_freq = inv_freq[jnp.arange(LANES) % half]
    ang = pos.astype(F32)[:, None] * lane_freq[None, :]
    tr = min(n, 1024)
    assert n % tr == 0 and tr % 8 == 0
    spec = pl.BlockSpec((tr, LANES), lambda i: (i, 0))
    return pl.pallas_call(
        _rope_kernel, grid=(n // tr,), in_specs=[spec], out_specs=[spec] * 3,
        out_shape=[jax.ShapeDtypeStruct((n, LANES), F32)] * 3,
        compiler_params=_params("arbitrary"), name="rope_tables",
    )(ang)


def _inproj_kernel(x_ref, mod_ref, n1w_ref, win_ref, gmat_ref, qkw_ref, cos_ref, sa_ref, sb_ref,
                   agb_ref, qt_ref, kt_ref, kb_ref, v_ref, vt_ref, z_even, z_odd,
                   *, aw, qc, vc, cw):
    tm = x_ref.shape[0]
    s = pl.program_id(0)

    @pl.when(s == 0)
    def _():
        z_odd[...] = jnp.zeros_like(z_odd)

    def step(z_w, z_r):
        x = x_ref[...]
        ms = jnp.mean(x * x, axis=-1, keepdims=True)
        u = x * lax.rsqrt(ms + EPS) * n1w_ref[...]
        u = u * (1.0 + mod_ref[1]) + mod_ref[0]
        z = jnp.dot(u.astype(BF16), win_ref[...], preferred_element_type=F32)
        ncols = z.shape[1]
        z_w[:, 0:ncols] = z
        o_q, o_v = 2 * aw, 2 * aw + 2 * qc
        o_gb = o_v + vc
        gm = gmat_ref[...]
        gw = gm.shape[0]
        for c0 in range(0, 2 * qc, gw):
            y = z[:, o_q + c0:o_q + c0 + gw]
            z_w[:, ncols + c0:ncols + c0 + gw] = jnp.dot((y * y).astype(BF16), gm,
                                                         preferred_element_type=F32)

        zcols = lambda a, b: z_r[:, a:b]

        za = zcols(0, 2 * aw)
        agb_ref[:, 0:aw] = za[:, 0:aw] * jax.nn.sigmoid(za[:, aw:2 * aw])

        v = zcols(o_v, o_v + vc)
        n_heads = vc // LANES
        for h in range(n_heads):
            vh = v[:, h * LANES:(h + 1) * LANES]
            v_ref[pl.ds(h, tm, stride=n_heads), :] = vh
            vt_ref[h * LANES:(h + 1) * LANES, :] = vh.T.astype(BF16)

        zg = zcols(o_gb, o_gb + 3 * cw)
        agb_ref[:, aw:aw + cw] = zg[:, cw:2 * cw] * zg[:, 2 * cw:3 * cw]
        agb_ref[:, aw + cw:aw + 2 * cw] = zg[:, 0:cw]

        cos, sa, sb = cos_ref[...], sa_ref[...], sb_ref[...]
        half = ROT_DIM // 2
        for c0 in range(0, 2 * qc, gw):
            y = zcols(o_q + c0, o_q + c0 + gw)
            gms = zcols(ncols + c0, ncols + c0 + gw)
            y = y * lax.rsqrt(gms + EPS) * qkw_ref[:, c0:c0 + gw]
            for c1 in range(0, gw, LANES):
                yc = y[:, c1:c1 + LANES]
                r = (yc * cos + pltpu.roll(yc, half, 1) * sa + pltpu.roll(yc, LANES - half, 1) * sb)
                col = c0 + c1
                if col < qc:
                    qt_ref[col:col + LANES, :] = (r * Q_SCALE).T.astype(BF16)
                else:
                    kt_ref[col - qc:col - qc + LANES, :] = r.T
                    kb_ref[:, col - qc:col - qc + LANES] = r.astype(BF16)

    @pl.when(s % 2 == 0)
    def _():
        step(z_even, z_odd)

    @pl.when(s % 2 == 1)
    def _():
        step(z_odd, z_even)


def _inproj(x, mod, n1w, win_b, gmat, qkw, tables, *, tm, tiles_per_seq, table_tiles):
    n, d = x.shape
    aw, qc = d // 4, d // 2
    vc, cw = d // 2, d - d // 4 - d // 2
    ncols = win_b.shape[1]
    r = mod.shape[2]
    nt = n // tm
    cur = lambda s: jnp.minimum(s, nt - 1)
    prv = lambda s: jnp.maximum(s - 1, 0)
    row = lambda w: pl.BlockSpec((tm, w), lambda s: (prv(s), 0))
    tab = pl.BlockSpec((tm, LANES), lambda s: (prv(s) % table_tiles, 0))
    tile_t = lambda w: pl.BlockSpec((None, w, tm), lambda s: (prv(s), 0, 0))
    kern = functools.partial(_inproj_kernel, aw=aw, qc=qc, vc=vc, cw=cw)
    return pl.pallas_call(
        kern,
        grid=(nt + 1,),
        in_specs=[pl.BlockSpec((tm, d), lambda s: (cur(s), 0)),
                  pl.BlockSpec((None, 6, r, d), lambda s: (cur(s) // tiles_per_seq, 0, 0, 0)),
                  _const_spec((1, d)), _const_spec((d, ncols)), _const_spec(gmat.shape),
                  _const_spec((1, 2 * qc)), tab, tab, tab],
        out_specs=[row(aw + 2 * cw), tile_t(qc),
                   pl.BlockSpec((None, qc, tm),
                                lambda s: (prv(s) // tiles_per_seq, 0, prv(s) % tiles_per_seq)),
                   row(qc), pl.BlockSpec((tm * (vc // LANES), LANES), lambda s: (prv(s), 0)),
                   tile_t(vc)],
        scratch_shapes=[pltpu.VMEM((tm, ncols + 2 * qc), F32)] * 2,
        out_shape=[jax.ShapeDtypeStruct((n, aw + 2 * cw), F32),
                   jax.ShapeDtypeStruct((n // tm, qc, tm), BF16),
                   jax.ShapeDtypeStruct((n // (tm * tiles_per_seq), qc, tm * tiles_per_seq), F32),
                   jax.ShapeDtypeStruct((n, qc), BF16),
                   jax.ShapeDtypeStruct((n * (vc // LANES), LANES), F32),
                   jax.ShapeDtypeStruct((n // tm, vc, tm), BF16)],
        compiler_params=_params("arbitrary"),
        name="inproj",
    )(x, mod, n1w, win_b, gmat, qkw, *tables)


def _lambda(lp_ref, lam_init):
    s1 = jnp.sum(lp_ref[0:1, :] * lp_ref[1:2, :], axis=-1, keepdims=True)
    s2 = jnp.sum(lp_ref[2:3, :] * lp_ref[3:4, :], axis=-1, keepdims=True)
    return jnp.exp(s1) - jnp.exp(s2) + lam_init


def _subln(o, w, lam_init):
    ms = jnp.mean(o * o, axis=-1, keepdims=True)
    return o * lax.rsqrt(ms + EPS) * w * (1.0 - lam_init)


_NT = (((1,), (1,)), ((), ()))


def _flash_kernel(qt_ref, k_ref, vt_ref, lp_ref, sw_ref, o_ref, qc_ref, sa_ref, sb_ref, xa_ref, xb_ref,
                  m_ref, l_ref, acc_ref, *, lam_init):
    i = pl.program_id(2)
    tk = k_ref.shape[0] // vt_ref.shape[0]
    hb = qt_ref.shape[0] // LANES
    streams = [(h, c) for h in range(hb) for c in range(2)]
    head = lambda h: slice(h * LANES, (h + 1) * LANES)
    for h in range(hb):
        qt = qt_ref[head(h), :]
        row = lax.broadcasted_iota(jnp.int32, qt.shape, 0)
        zero = jnp.zeros_like(qt)
        qc_ref[2 * h] = jnp.where(row < QK_DIM, qt, zero)
        qc_ref[2 * h + 1] = jnp.where(row >= QK_DIM, qt, zero)
    m_ref[...] = jnp.full_like(m_ref, NEG_INF)
    l_ref[...] = jnp.zeros_like(l_ref)
    acc_ref[...] = jnp.zeros_like(acc_ref)

    def produce(j, s_ref, x_ref):
        rows = pl.ds(pl.multiple_of(j * tk, tk), tk)
        for n, (h, c) in enumerate(streams):
            s = jnp.dot(k_ref[rows, head(h)], qc_ref[n], preferred_element_type=F32)
            s_ref[n] = s
            x_ref[n] = jnp.max(s, axis=0, keepdims=True)

    def consume(j, s_ref, x_ref, diag):
        for n, (h, c) in enumerate(streams):
            s = s_ref[n]
            if diag:
                kr = lax.broadcasted_iota(jnp.int32, s.shape, 0)
                qq = lax.broadcasted_iota(jnp.int32, s.shape, 1)
                s = jnp.where(kr <= qq, s, NEG_INF)
                x = jnp.max(s, axis=0, keepdims=True)
            else:
                x = x_ref[n]
            m_prev = m_ref[n]
            m_new = jnp.maximum(m_prev, x)
            alpha = jnp.exp2(m_prev - m_new)
            p = jnp.exp2(s - m_new)
            l_ref[n] = alpha * l_ref[n] + jnp.sum(p, axis=0, keepdims=True)
            acc_ref[n] = alpha * acc_ref[n] + jnp.dot(vt_ref[j, head(h), :], p.astype(BF16),
                                                      preferred_element_type=F32)
            m_ref[n] = m_new

    produce(0, sa_ref, xa_ref)

    def pair(t, carry):
        j = 2 * t
        produce(j + 1, sb_ref, xb_ref)
        consume(j, sa_ref, xa_ref, False)
        produce(j + 2, sa_ref, xa_ref)
        consume(j + 1, sb_ref, xb_ref, False)
        return carry

    lax.fori_loop(0, i // 2, pair, 0)

    @pl.when(i % 2 == 0)
    def _():
        consume(i, sa_ref, xa_ref, True)

    @pl.when(i % 2 == 1)
    def _():
        produce(i, sb_ref, xb_ref)
        consume(i - 1, sa_ref, xa_ref, False)
        consume(i, sb_ref, xb_ref, True)

    lam = _lambda(lp_ref, lam_init)
    for h in range(hb):
        ot = acc_ref[2 * h] / l_ref[2 * h] - lam * (acc_ref[2 * h + 1] / l_ref[2 * h + 1])
        o_ref[:, head(h)] = _subln(ot.T, sw_ref[...], lam_init).astype(o_ref.dtype)


def _flash(qt, kb, vt, lp, sw, *, batch, seq, lam_init):
    n, hc = kb.shape
    heads = hc // LANES
    hb = FLASH_HEADS if heads % FLASH_HEADS == 0 else 1
    tq, tk = qt.shape[2], vt.shape[2]
    assert tq == tk
    nq, nk = seq // tq, seq // tk
    cspec = lambda shape: pl.BlockSpec(shape, lambda b, h, i: (0, 0))
    stat = pltpu.VMEM((2 * hb, 1, tq), F32)
    score = pltpu.VMEM((2 * hb, tk, tq), F32)
    return pl.pallas_call(
        functools.partial(_flash_kernel, lam_init=lam_init),
        grid=(batch, heads // hb, nq),
        in_specs=[pl.BlockSpec((None, hb * LANES, tq), lambda b, h, i: (b * nq + i, h, 0)),
                  pl.BlockSpec((seq, hb * LANES), lambda b, h, i: (b, h)),
                  pl.BlockSpec((nk, hb * LANES, tk), lambda b, h, i: (b, h, 0)),
                  cspec((8, LANES)), cspec((1, LANES))],
        out_specs=pl.BlockSpec((tq, hb * LANES), lambda b, h, i: (b * nq + i, h)),
        scratch_shapes=[pltpu.VMEM((2 * hb, LANES, tq), BF16), score, score, stat, stat, stat, stat,
                        pltpu.VMEM((2 * hb, V_DIM, tq), F32)],
        out_shape=jax.ShapeDtypeStruct((n, hc), BF16),
        compiler_params=_params("arbitrary", "arbitrary", "arbitrary"),
        name="flash_diff_attn",
    )(qt, kb, vt, lp, sw)


def _paged_kernel(pt_ref, q_ref, kn_ref, vn_ref, lp_ref, sw_ref, kt_hbm, v_hbm, o_ref,
                  kbuf, vbuf, ksem, vsem, m, l, acc, *, layer, pages, n_new, lam_init, heads):
    c = pl.program_id(1)
    g = pl.program_id(0) * pl.num_programs(1) + c
    n_steps = pl.num_programs(0) * pl.num_programs(1)
    page = kbuf.shape[3]
    rows_h = 2 * 8

    def copies(step):
        slot = step % PAGE_BUFFERS
        out = []
        for i in range(pages):
            pidx = pt_ref[step * pages + i]
            out.append(pltpu.make_async_copy(kt_hbm.at[layer, pidx], kbuf.at[slot, i], ksem.at[slot]))
            out.append(pltpu.make_async_copy(v_hbm.at[layer, pidx], vbuf.at[slot, i], vsem.at[slot]))
        return out

    @pl.when(g == 0)
    def _():
        for ahead in range(PAGE_BUFFERS - 1):
            @pl.when(ahead < n_steps)
            def _():
                for cp in copies(ahead):
                    cp.start()

    @pl.when(g + PAGE_BUFFERS - 1 < n_steps)
    def _():
        for cp in copies(g + PAGE_BUFFERS - 1):
            cp.start()

    for cp in copies(g):
        cp.wait()
    slot = g % PAGE_BUFFERS
    k_refs = [kbuf.at[slot, i] for i in range(pages)]
    v_refs = [vbuf.at[slot, i] for i in range(pages)]

    @pl.when(c == 0)
    def _():
        m[...] = jnp.full_like(m, NEG_INF)
        l[...] = jnp.zeros_like(l)
        acc[...] = jnp.zeros_like(acc)

    def update(s, v_of_head):
        m_prev = m[...]
        m_new = jnp.maximum(m_prev, jnp.max(s, axis=-1, keepdims=True))
        alpha = jnp.exp2(m_prev - m_new)
        p = jnp.exp2(s - m_new)
        l[...] = alpha * l[...] + jnp.sum(p, axis=-1, keepdims=True)
        m[...] = m_new
        pb = p.astype(BF16)
        for h in range(heads):
            r = slice(h * rows_h, (h + 1) * rows_h)
            acc[r, :] = alpha[r, :] * acc[r, :] + jnp.dot(pb[r, :], v_of_head(h),
                                                          preferred_element_type=F32)

    q = q_ref[...]
    kcat = jnp.concatenate([r[...].astype(BF16) for r in k_refs], axis=1)
    s = jnp.dot(q, kcat, preferred_element_type=F32)
    update(s, lambda h: jnp.concatenate(
        [r[pl.ds(h, page, stride=heads), :].astype(BF16) for r in v_refs], axis=0))

    @pl.when(c == pl.num_programs(1) - 1)
    def _():
        s = lax.dot_general(q, kn_ref[...].astype(BF16), _NT, preferred_element_type=F32)
        t_q = lax.broadcasted_iota(jnp.int32, s.shape, 0) % 8
        t_k = lax.broadcasted_iota(jnp.int32, s.shape, 1)
        s = jnp.where((t_k <= t_q) & (t_k < n_new), s, NEG_INF)
        update(s, lambda h: vn_ref[:, h * LANES:(h + 1) * LANES].astype(BF16))
        lam = _lambda(lp_ref, lam_init)
        for h in range(heads):
            r1 = slice(h * rows_h, h * rows_h + 8)
            r2 = slice(h * rows_h + 8, (h + 1) * rows_h)
            o = acc[r1, :] / l[r1, :] - lam * (acc[r2, :] / l[r2, :])
            o_ref[:, h * LANES:(h + 1) * LANES] = _subln(o, sw_ref[...], lam_init)


def _paged(qblk, knew, vnew, kt_pages, v_pages, page_table, lp, sw, *, layer, n_new, lam_init):
    db, rows, hc = qblk.shape
    heads = hc // LANES
    n_pages = page_table.shape[1]
    page = kt_pages.shape[3]
    pages = math.gcd(n_pages, 8)
    chunks = n_pages // pages
    pt = page_table.reshape(-1).astype(jnp.int32)

    per_seq = lambda r: pl.BlockSpec((None, r, hc), lambda b, c, pt: (b, 0, 0))
    cspec = lambda shape: pl.BlockSpec(shape, lambda b, c, pt: (0, 0))
    hbm = pl.BlockSpec(memory_space=pl.ANY)
    kern = functools.partial(_paged_kernel, layer=layer, pages=pages, n_new=n_new,
                             lam_init=lam_init, heads=heads)
    return pl.pallas_call(
        kern,
        grid_spec=pltpu.PrefetchScalarGridSpec(
            num_scalar_prefetch=1,
            grid=(db, chunks),
            in_specs=[per_seq(rows), per_seq(knew.shape[1]), per_seq(vnew.shape[1]),
                      cspec((8, LANES)), cspec((1, LANES)), hbm, hbm],
            out_specs=per_seq(8),
            scratch_shapes=[pltpu.VMEM((PAGE_BUFFERS, pages, hc, page), F32),
                            pltpu.VMEM((PAGE_BUFFERS, pages, page * heads, LANES), F32),
                            pltpu.SemaphoreType.DMA((PAGE_BUFFERS,)),
                            pltpu.SemaphoreType.DMA((PAGE_BUFFERS,)),
                            pltpu.VMEM((rows, 1), F32), pltpu.VMEM((rows, 1), F32),
                            pltpu.VMEM((rows, LANES), F32)]),
        out_shape=jax.ShapeDtypeStruct((db, 8, hc), F32),
        compiler_params=_params("arbitrary", "arbitrary"),
        name="paged_diff_attn",
    )(pt, qblk, knew, vnew, lp, sw, kt_pages, v_pages)


def _mix_tail(x, ya_pre, yc_pre, gb, yb, mod_ref, cab_ref, lnw_ref, lnb_ref, wout_ref, n2w_ref,
              x1_ref, h_ref, *, aw, bw):
    ya = ya_pre + cab_ref[...]
    mu = jnp.mean(ya, axis=-1, keepdims=True)
    var = jnp.mean(jnp.square(ya - mu), axis=-1, keepdims=True)
    ya = (ya - mu) * lax.rsqrt(var + EPS) * lnw_ref[...] + lnb_ref[...]
    ya = ya * jax.nn.sigmoid(ya)
    yc = gb * yc_pre
    y = jnp.dot(ya.astype(BF16), wout_ref[0:aw, :], preferred_element_type=F32)
    y += jnp.dot(yb.astype(BF16), wout_ref[aw:aw + bw, :], preferred_element_type=F32)
    y += jnp.dot(yc.astype(BF16), wout_ref[aw + bw:, :], preferred_element_type=F32)
    x1 = x + mod_ref[2] * y
    x1_ref[...] = x1
    ms = jnp.mean(x1 * x1, axis=-1, keepdims=True)
    h = x1 * lax.rsqrt(ms + EPS) * n2w_ref[...]
    h_ref[...] = (h * (1.0 + mod_ref[4]) + mod_ref[3]).astype(h_ref.dtype)


def _mixmlp_kernel(x_ref, mod_ref, modp_ref, agb_ref, halo_ref, yb_ref, caw_ref, cab_ref, lnw_ref,
                   lnb_ref, ccw_ref, wout_ref, n2w_ref, wup_ref, wdn_ref, o_ref, buf, sh,
                   x1_even, h_even, x1_odd, h_odd, *, aw, cw, bw, tiles_per_seq, fc):
    s = pl.program_id(0)
    tm = x_ref.shape[0]

    @pl.when(s == 0)
    def _():
        x1_odd[...] = jnp.zeros_like(x1_odd)
        h_odd[...] = jnp.zeros_like(h_odd)
        buf[HALO + tm:, :] = jnp.zeros((8, aw + cw), F32)

    def step(x1_w, h_w, x1_r, h_r):
        tile = jnp.minimum(s, pl.num_programs(0) - 2)
        first = tile % tiles_per_seq == 0
        halo = halo_ref[:, 0:aw + cw]
        buf[0:HALO, :] = jnp.where(first, jnp.zeros_like(halo), halo)
        buf[HALO:HALO + tm, :] = agb_ref[:, 0:aw + cw]
        for b in range(8):
            sh[b] = buf[b:b + tm + HALO, 0:aw]
        ya = jnp.zeros((tm, aw), F32)
        for j in range(KA):
            a8, b = divmod(HALO - (KA - 1) + j, 8)
            ya += caw_ref[j:j + 1, :] * sh[b, 8 * a8:8 * a8 + tm, :]
        yc = jnp.zeros((tm, cw), F32)
        for j in range(KC):
            off = HALO - (KC - 1) + j
            yc += ccw_ref[j:j + 1, :] * buf[off:off + tm, aw:aw + cw]
        gb = agb_ref[:, aw + cw:aw + 2 * cw]
        _mix_tail(x_ref[...], ya, yc, gb, yb_ref[...], mod_ref, cab_ref, lnw_ref, lnb_ref, wout_ref,
                  n2w_ref, x1_w, h_w, aw=aw, bw=bw)
        o_ref[...] = _mlp_body(x1_r[...], h_r[...], modp_ref[5], wup_ref, wdn_ref, fc)

    @pl.when(s % 2 == 0)
    def _():
        step(x1_even, h_even, x1_odd, h_odd)

    @pl.when(s % 2 == 1)
    def _():
        step(x1_odd, h_odd, x1_even, h_even)


def _mix_hist_kernel(x_ref, mod_ref, af_ref, gf_ref, gb_ref, yb_ref, caw_ref, cab_ref, lnw_ref,
                     lnb_ref, ccw_ref, wout_ref, n2w_ref, x1_ref, h_ref, *, aw, cw, bw):
    ts = af_ref.shape[0] - (KA - 1)
    nb = af_ref.shape[1]
    ya = jnp.zeros((ts, nb, aw), F32)
    for j in range(KA):
        ya += caw_ref[j:j + 1, :][None] * af_ref[j:j + ts]
    yc = jnp.zeros((ts, nb, cw), F32)
    for j in range(KC):
        yc += ccw_ref[j:j + 1, :][None] * gf_ref[j:j + ts]
    _mix_tail(x_ref[...], ya.reshape(ts * nb, aw), yc.reshape(ts * nb, cw), gb_ref[...],
              yb_ref[...], mod_ref, cab_ref, lnw_ref, lnb_ref, wout_ref, n2w_ref, x1_ref, h_ref,
              aw=aw, bw=bw)


def _mix_weight_specs(lw, d, aw, cw):
    return [_const_spec((KA, aw)), _const_spec((1, aw)), _const_spec((1, aw)), _const_spec((1, aw)),
            _const_spec((KC, cw)), _const_spec((d, d)), _const_spec((1, d))]


def _mix_weights(lw):
    return (lw["conv_a_w"], lw["conv_a_b"], lw["ln_a_w"], lw["ln_a_b"], lw["conv_c_w"],
            lw["w_out_b"], lw["norm2_w"])


def _mixmlp(x, mod, agb, yb, lw, wup_b, wdn_b, *, tm, tiles_per_seq):
    n, d = x.shape
    f = wup_b.shape[1]
    aw, bw = d // 4, d // 2
    cw = d - aw - bw
    nt = n // tm
    cur = lambda s: jnp.minimum(s, nt - 1)
    prv = lambda s: jnp.maximum(s - 1, 0)
    row = lambda w: pl.BlockSpec((tm, w), lambda s: (cur(s), 0))
    modspec = lambda t: pl.BlockSpec((None, 6, 1, d), lambda s: (t(s) // tiles_per_seq, 0, 0, 0))
    hpt = tm // HALO
    halo = pl.BlockSpec((HALO, aw + 2 * cw), lambda s: (jnp.maximum(cur(s) * hpt - 1, 0), 0))
    kern = functools.partial(_mixmlp_kernel, aw=aw, cw=cw, bw=bw, tiles_per_seq=tiles_per_seq,
                             fc=min(f, 1024))
    return pl.pallas_call(
        kern,
        grid=(nt + 1,),
        in_specs=[row(d), modspec(cur), modspec(prv), row(aw + 2 * cw), halo, row(bw)]
                 + _mix_weight_specs(lw, d, aw, cw) + [_const_spec((d, f)), _const_spec((f, d))],
        out_specs=pl.BlockSpec((tm, d), lambda s: (prv(s), 0)),
        out_shape=jax.ShapeDtypeStruct((n, d), F32),
        scratch_shapes=[pltpu.VMEM((HALO + tm + 8, aw + cw), F32),
                        pltpu.VMEM((8, HALO + tm, aw), F32),
                        pltpu.VMEM((tm, d), F32), pltpu.VMEM((tm, d), BF16),
                        pltpu.VMEM((tm, d), F32), pltpu.VMEM((tm, d), BF16)],
        compiler_params=_params("arbitrary"),
        name="mix_mlp",
    )(x, mod, mod, agb, agb, yb, *_mix_weights(lw), wup_b, wdn_b)


def _mix_hist(x, mod, af, gf, gb, yb, lw):
    n, d = x.shape
    aw, bw = d // 4, d // 2
    cw = d - aw - bw
    full = lambda a: pl.BlockSpec(a.shape, lambda i: (0,) * a.ndim)
    kern = functools.partial(_mix_hist_kernel, aw=aw, cw=cw, bw=bw)
    return pl.pallas_call(
        kern,
        grid=(1,),
        in_specs=[full(x), pl.BlockSpec((None,) + mod.shape[1:], lambda i: (0, 0, 0, 0)),
                  full(af), full(gf), full(gb), full(yb)] + _mix_weight_specs(lw, d, aw, cw),
        out_specs=[pl.BlockSpec((n, d), lambda i: (0, 0))] * 2,
        out_shape=[jax.ShapeDtypeStruct((n, d), F32), jax.ShapeDtypeStruct((n, d), BF16)],
        compiler_params=_params("arbitrary"),
        name="mix_hist",
    )(x, mod, af, gf, gb, yb, *_mix_weights(lw))


def _mlp_body(x1, h, gate2, wup_ref, wdn_ref, fc):
    f = wup_ref.shape[1]
    y = jnp.zeros(x1.shape, F32)
    for c0 in range(0, f, fc):
        hid = jnp.dot(h, wup_ref[:, c0:c0 + fc], preferred_element_type=F32)
        hid = jnp.square(jnp.maximum(hid, 0.0)).astype(BF16)
        y += jnp.dot(hid, wdn_ref[c0:c0 + fc, :], preferred_element_type=F32)
    return x1 + gate2 * y


def _mlp_kernel(x_ref, h_ref, mod_ref, wup_ref, wdn_ref, o_ref, *, fc):
    o_ref[...] = _mlp_body(x_ref[...], h_ref[...], mod_ref[5], wup_ref, wdn_ref, fc)


def _mlp(x1, h, mod, wup_b, wdn_b, *, tm, tiles_per_seq):
    n, d = x1.shape
    f = wup_b.shape[1]
    r = mod.shape[2]
    row = pl.BlockSpec((tm, d), lambda i: (i, 0))
    return pl.pallas_call(
        functools.partial(_mlp_kernel, fc=min(f, 1024)),
        grid=(n // tm,),
        in_specs=[row, row,
                  pl.BlockSpec((None, 6, r, d), lambda i: (i // tiles_per_seq, 0, 0, 0)),
                  _const_spec((d, f)), _const_spec((f, d))],
        out_specs=row,
        out_shape=jax.ShapeDtypeStruct((n, d), F32),
        compiler_params=_params("arbitrary"),
        name="mlp",
    )(x1, h, mod, wup_b, wdn_b)


def _group_matrix(width):
    g = jnp.arange(width) // QK_DIM
    return jnp.where(g[:, None] == g[None, :], 1.0 / QK_DIM, 0.0).astype(BF16)


def kernel(x_prompt, x_sample, cache_k, cache_v, state_conv_a, state_conv_c, page_table, c_prompt, c_sample, norm1_w, norm2_w, w_ada, b_ada, w_in, conv_a_w, conv_a_b, ln_a_w, ln_a_b, q_norm_w, k_norm_w, lambda_q1, lambda_k1, lambda_q2, lambda_k2, subln_w, conv_c_w, w_out, w_mlp_up, w_mlp_down):
    depth = w_in.shape[0]
    bp, seq, d = x_prompt.shape
    db, ts, _ = x_sample.shape
    aw, bw = d // 4, d // 2
    cw = d - aw - bw
    heads = bw // V_DIM
    hc = heads * V_DIM
    n_pages, page = page_table.shape[1], cache_k.shape[2]
    past = n_pages * page
    tm = min(seq, 512)
    assert seq % tm == 0 and tm % HALO == 0 and (ts * db) % 8 == 0 and db % 8 == 0
    tiles = seq // tm

    n_c = bp + db
    n_cp = -(-n_c // 8) * 8
    c_all = jnp.zeros((n_cp, d), F32).at[:n_c].set(jnp.concatenate([c_prompt, c_sample], 0))
    mod_all = _ada_mod(c_all, w_ada, b_ada).reshape(depth, n_cp, 6, d)
    mod_p = mod_all[:, :bp, :, None, :]
    mod_s = jnp.tile(mod_all[:, bp:n_c], (1, ts, 1, 1)).transpose(0, 2, 1, 3)[:, None]

    tab_p = _rope_tables(jnp.arange(seq))
    pos_s = jnp.repeat(past + jnp.arange(ts), db)
    tab_s = _rope_tables(pos_s)

    gmat = _group_matrix(2 * LANES)
    kt_pages = cache_k.transpose(0, 1, 3, 4, 5, 2).reshape(depth, cache_k.shape[1], hc, page)
    v_pages = cache_v.reshape(depth, cache_v.shape[1], page * heads, V_DIM)

    lane = jnp.arange(hc)
    sel = (lane[None, None, :] // QK_DIM
           == (jnp.arange(heads)[:, None, None] * 2 + jnp.arange(2)[None, :, None])).astype(BF16)

    xp = x_prompt.reshape(bp * seq, d)
    xs = x_sample.transpose(1, 0, 2).reshape(ts * db, d)
    outs = {k: [] for k in ("kp", "vp", "ap", "cp", "ks", "vs", "as", "cs")}
    for l in range(depth):
        lam_init = 0.8 - 0.6 * math.exp(-0.3 * l)
        lw = {"conv_a_w": conv_a_w[l], "conv_a_b": conv_a_b[l][None], "ln_a_w": ln_a_w[l][None],
              "ln_a_b": ln_a_b[l][None], "conv_c_w": conv_c_w[l], "w_out_b": w_out[l].astype(BF16),
              "norm2_w": norm2_w[l][None]}
        win_b = w_in[l].astype(BF16)
        wup_b, wdn_b = w_mlp_up[l].astype(BF16), w_mlp_down[l].astype(BF16)
        n1w = norm1_w[l][None]
        qkw = jnp.concatenate([jnp.tile(q_norm_w[l], 2 * heads), jnp.tile(k_norm_w[l], 2 * heads)])[None]
        lp = jnp.zeros((8, LANES), F32).at[:4, :QK_DIM].set(
            jnp.stack([lambda_q1[l], lambda_k1[l], lambda_q2[l], lambda_k2[l]]))
        sw = subln_w[l][None]

        agb, qt, kt, kb, v, vt = _inproj(xp, mod_p[l], n1w, win_b, gmat, qkw, tab_p,
                                         tm=tm, tiles_per_seq=tiles, table_tiles=tiles)
        yb = _flash(qt, kb, vt, lp, sw, batch=bp, seq=seq, lam_init=lam_init)
        xp = _mixmlp(xp, mod_p[l], agb, yb, lw, wup_b, wdn_b, tm=tm, tiles_per_seq=tiles)
        outs["kp"].append(kt.reshape(bp, heads, 2, QK_DIM, seq).transpose(0, 4, 1, 2, 3))
        outs["vp"].append(v.reshape(bp, seq, heads, V_DIM))
        agb3 = agb.reshape(bp, seq, aw + 2 * cw)
        if seq >= KA - 1:
            outs["ap"].append(agb3[:, seq - (KA - 1):, :aw])
            outs["cp"].append(agb3[:, seq - (KC - 1):, aw:aw + cw])
        else:
            za = jnp.zeros((bp, KA - 1 - seq, aw), F32)
            zc = jnp.zeros((bp, max(KC - 1 - seq, 0), cw), F32)
            outs["ap"].append(jnp.concatenate([za, agb3[..., :aw]], 1))
            outs["cp"].append(jnp.concatenate([zc, agb3[..., aw:aw + cw]], 1)[:, -(KC - 1):])

        n_s = ts * db
        agb, qt, kt, kb, v, vt = _inproj(xs, mod_s[l], n1w, win_b, gmat, qkw, tab_s,
                                         tm=n_s, tiles_per_seq=1, table_tiles=1)
        k_seq = kt[0].T.reshape(ts, db, hc).transpose(1, 0, 2)
        v_seq = v.reshape(ts, db, hc).transpose(1, 0, 2)
        q_seq = qt[0].T.reshape(ts, db, hc).transpose(1, 0, 2)
        q_pad = jnp.zeros((db, 8, hc), BF16).at[:, :ts].set(q_seq)
        qblk = (q_pad[:, None, None] * sel[None, :, :, None, :]).reshape(db, 2 * heads * 8, hc)
        knew = jnp.zeros((db, LANES, hc), F32).at[:, :ts].set(k_seq)
        vnew = jnp.zeros((db, LANES, hc), F32).at[:, :ts].set(v_seq)
        o = _paged(qblk, knew, vnew, kt_pages, v_pages, page_table, lp, sw,
                   layer=l, n_new=ts, lam_init=lam_init)
        yb = o[:, :ts].transpose(1, 0, 2).reshape(n_s, hc)
        af = jnp.concatenate([state_conv_a[l].transpose(1, 0, 2), agb[:, :aw].reshape(ts, db, aw)], 0)
        gf = jnp.concatenate([state_conv_c[l].transpose(1, 0, 2),
                              agb[:, aw:aw + cw].reshape(ts, db, cw)], 0)
        x1, h = _mix_hist(xs, mod_s[l], af, gf, agb[:, aw + cw:], yb, lw)
        xs = _mlp(x1, h, mod_s[l], wup_b, wdn_b, tm=n_s, tiles_per_seq=1)
        outs["ks"].append(k_seq.reshape(db, ts, heads, 2, QK_DIM))
        outs["vs"].append(v_seq.reshape(db, ts, heads, V_DIM))
        outs["as"].append(af[-(KA - 1):].transpose(1, 0, 2))
        outs["cs"].append(gf[-(KC - 1):].transpose(1, 0, 2))

    st = lambda key: jnp.stack(outs[key])
    return (xp.reshape(bp, seq, d), xs.reshape(ts, db, d).transpose(1, 0, 2),
            st("kp"), st("vp"), st("ap"), st("cp"), st("ks"), st("vs"), st("as"), st("cs"))
```
